```python
import math
import jax, jax.numpy as jnp
from jax import lax
import numpy as np

D_MODEL = 4096
BATCH = 2
SEQ = 8192
DEPTH = 4

SSM_WIDTH = D_MODEL // 2
SSM_GROUP = 16
SSM_GROUPS = SSM_WIDTH // SSM_GROUP
SSM_STATE = 64
SCAN_CHUNK = 128
DT_MIN, DT_MAX = 1e-3, 1e-1
POOL_WIDTH = D_MODEL // 2
POOL_WINDOWS = (2, 4, 8, 16)
POOL_GROUPS = len(POOL_WINDOWS)
POOL_GROUP_WIDTH = POOL_WIDTH // POOL_GROUPS
IN_WIDTH = SSM_WIDTH + POOL_WIDTH + 2 * D_MODEL
D_FF = 2 * D_MODEL
CONV_WIDTH = 3
N_MOD = 6
RMS_EPS = 1e-6

kernel_name = "hybrid_s5_pool_gated_convffn_adaln"


def rmsnorm(x, g):
    xf = x.astype(jnp.float32)
    y = xf * lax.rsqrt(jnp.mean(xf * xf, axis=-1, keepdims=True) + RMS_EPS)
    return (y * g.astype(jnp.float32)).astype(x.dtype)


def _cmul(ar, ai, br, bi):
    return ar * br - ai * bi, ar * bi + ai * br


def _scan_combine(e1, e2):
    a1r, a1i, b1r, b1i = e1
    a2r, a2i, b2r, b2i = e2
    ar, ai = _cmul(a2r, a2i, a1r, a1i)
    br, bi = _cmul(a2r, a2i, b1r, b1i)
    return ar, ai, br + b2r, bi + b2i


def s5_mixer(u, a_re, a_im, log_dt, b_re, b_im, c_re, c_im, d_skip, w_glu, b_glu):
    f32 = jnp.float32
    bsz, seq, _ = u.shape
    lam_re = jnp.minimum(a_re.astype(f32), -1e-4)
    lam_im = a_im.astype(f32)
    dt = jnp.exp(log_dt.astype(f32))[:, None]
    mag = jnp.exp(lam_re * dt)
    abar_re = mag * jnp.cos(lam_im * dt)
    abar_im = mag * jnp.sin(lam_im * dt)
    den = lam_re * lam_re + lam_im * lam_im
    x_re = abar_re - 1.0
    f_re = (x_re * lam_re + abar_im * lam_im) / den
    f_im = (abar_im * lam_re - x_re * lam_im) / den
    bb_re, bb_im = _cmul(f_re[..., None], f_im[..., None], b_re.astype(f32), b_im.astype(f32))
    cr = c_re.astype(f32)
    ci = c_im.astype(f32)

    n_chunks = seq // SCAN_CHUNK
    uf = u.astype(f32)
    u_c = jnp.swapaxes(uf.reshape(bsz, n_chunks, SCAN_CHUNK, SSM_GROUPS, SSM_GROUP), 0, 1)
    shape_bcgn = (bsz, SCAN_CHUNK, SSM_GROUPS, SSM_STATE)
    a_seq_re = jnp.broadcast_to(abar_re, shape_bcgn)
    a_seq_im = jnp.broadcast_to(abar_im, shape_bcgn)

    def chunk_step(carry, uc):
        h0_re, h0_im = carry
        bu_re = jnp.einsum('bcgh,gnh->bcgn', uc, bb_re)
        bu_im = jnp.einsum('bcgh,gnh->bcgn', uc, bb_im)
        p_re, p_im, s_re, s_im = lax.associative_scan(
            _scan_combine, (a_seq_re, a_seq_im, bu_re, bu_im), axis=1)
        in_re, in_im = _cmul(p_re, p_im, h0_re[:, None], h0_im[:, None])
        s_re = s_re + in_re
        s_im = s_im + in_im
        y = (jnp.einsum('bcgn,ghn->bcgh', s_re, cr)
             - jnp.einsum('bcgn,ghn->bcgh', s_im, ci))
        return (s_re[:, -1], s_im[:, -1]), y

    init = (jnp.zeros((bsz, SSM_GROUPS, SSM_STATE), f32),
            jnp.zeros((bsz, SSM_GROUPS, SSM_STATE), f32))
    _, ys = lax.scan(chunk_step, init, u_c)
    y = jnp.swapaxes(ys, 0, 1).reshape(bsz, seq, SSM_WIDTH)
    y = y + uf * d_skip.astype(f32).reshape(-1)
    y = jax.nn.gelu(y).astype(u.dtype)
    return y * jax.nn.sigmoid(y @ w_glu + b_glu)


def pool_mixer(v, w_pool, b_pool, pool_scale):
    f32 = jnp.float32
    bsz, seq, _ = v.shape
    vf = v.astype(f32).reshape(bsz, seq, POOL_GROUPS, POOL_GROUP_WIDTH)
    csum = jnp.cumsum(vf, axis=1)
    pos = jnp.arange(seq)
    outs = []
    for g, win in enumerate(POOL_WINDOWS):
        cs = csum[:, :, g]
        lag = jnp.pad(cs, ((0, 0), (win, 0), (0, 0)))[:, :seq]
        cnt = jnp.minimum(pos + 1, win).astype(f32)[None, :, None]
        outs.append((cs - lag) / cnt - vf[:, :, g])
    pooled = jnp.stack(outs, axis=2).astype(v.dtype)
    mixed = jnp.einsum('bsgc,gcd->bsgd', pooled, w_pool) + b_pool
    mixed = mixed * pool_scale.reshape(POOL_GROUPS, POOL_GROUP_WIDTH)
    return mixed.reshape(bsz, seq, POOL_WIDTH)


def causal_dwconv(h, w, b):
    seq = h.shape[1]
    hp = jnp.pad(h, ((0, 0), (CONV_WIDTH - 1, 0), (0, 0)))
    out = b
    for k in range(CONV_WIDTH):
        out = out + w[k] * hp[:, k:k + seq]
    return out


def conv_gated_mlp(h, w_up, conv_w, conv_b, w_down):
    up = causal_dwconv(h @ w_up, conv_w, conv_b)
    gate, val = jnp.split(up, 2, axis=-1)
    return (jax.nn.silu(gate) * val) @ w_down


def setup_inputs(seed: int = 0) -> dict:
    key = jax.random.key(seed)
    ks = jax.random.split(key, 32)
    nrm = jax.random.normal
    f32 = jnp.float32
    L, D, G, N, H = DEPTH, D_MODEL, SSM_GROUPS, SSM_STATE, SSM_GROUP
    return {
        "x": nrm(ks[0], (BATCH, SEQ, D), f32),
        "c": nrm(ks[1], (BATCH, D), f32),
        "w_cond": nrm(ks[2], (D, N_MOD * D), f32) * (0.5 * D ** -0.5),
        "b_cond": nrm(ks[3], (N_MOD * D,), f32) * 0.01,
        "ada_table": nrm(ks[4], (L, N_MOD, D), f32) * 0.1,
        "norm1_g": 1.0 + 0.02 * nrm(ks[5], (L, D), f32),
        "norm2_g": 1.0 + 0.02 * nrm(ks[6], (L, D), f32),
        "w_in": nrm(ks[7], (L, D, IN_WIDTH), f32) * D ** -0.5,
        "ssm_a_re": -0.5 + 0.01 * nrm(ks[8], (L, G, N), f32),
        "ssm_a_im": math.pi * jnp.arange(N, dtype=f32)[None, None, :] + 0.01 * nrm(ks[9], (L, G, N), f32),
        "ssm_log_dt": jax.random.uniform(ks[10], (L, G), f32, math.log(DT_MIN), math.log(DT_MAX)),
        "ssm_b_re": nrm(ks[11], (L, G, N, H), f32) * (2 * H) ** -0.5,
        "ssm_b_im": nrm(ks[12], (L, G, N, H), f32) * (2 * H) ** -0.5,
        "ssm_c_re": nrm(ks[13], (L, G, H, N), f32) * N ** -0.5,
        "ssm_c_im": nrm(ks[14], (L, G, H, N), f32) * N ** -0.5,
        "ssm_d": nrm(ks[15], (L, G, H), f32),
        "w_glu": nrm(ks[16], (L, SSM_WIDTH, SSM_WIDTH), f32) * SSM_WIDTH ** -0.5,
        "b_glu": nrm(ks[17], (L, SSM_WIDTH), f32) * 0.01,
        "w_pool": nrm(ks[18], (L, POOL_GROUPS, POOL_GROUP_WIDTH, POOL_GROUP_WIDTH), f32) * POOL_GROUP_WIDTH ** -0.5,
        "b_pool": nrm(ks[19], (L, POOL_GROUPS, POOL_GROUP_WIDTH), f32) * 0.01,
        "pool_scale": 1.0 + 0.1 * nrm(ks[20], (L, POOL_WIDTH), f32),
        "w_ssm_out": nrm(ks[21], (L, SSM_WIDTH, D), f32) * SSM_WIDTH ** -0.5,
        "w_pool_out": nrm(ks[22], (L, POOL_WIDTH, D), f32) * POOL_WIDTH ** -0.5,
        "w_o": nrm(ks[23], (L, D, D), f32) * D ** -0.5,
        "w_up": nrm(ks[24], (L, D, 2 * D_FF), f32) * D ** -0.5,
        "conv_w": nrm(ks[25], (L, CONV_WIDTH, 2 * D_FF), f32) * CONV_WIDTH ** -0.5,
        "conv_b": nrm(ks[26], (L, 2 * D_FF), f32) * 0.01,
        "w_down": nrm(ks[27], (L, D_FF, D), f32) * D_FF ** -0.5,
        "final_g": 1.0 + 0.02 * nrm(ks[28], (D,), f32),
    }


def reference(x, c, w_cond, b_cond, ada_table, norm1_g, norm2_g, w_in,
              ssm_a_re, ssm_a_im, ssm_log_dt, ssm_b_re, ssm_b_im, ssm_c_re, ssm_c_im, ssm_d,
              w_glu, b_glu, w_pool, b_pool, pool_scale, w_ssm_out, w_pool_out, w_o,
              w_up, conv_w, conv_b, w_down, final_g):
    bsz = x.shape[0]
    cond = (jax.nn.silu(c) @ w_cond + b_cond).reshape(bsz, N_MOD, D_MODEL)
    split_at = [SSM_WIDTH, SSM_WIDTH + POOL_WIDTH, SSM_WIDTH + POOL_WIDTH + D_MODEL]
    h = x
    for l in range(DEPTH):
        mod = (cond + ada_table[l])[:, :, None, :]
        shift1, scale1, gate1 = mod[:, 0], mod[:, 1], mod[:, 2]
        shift2, scale2, gate2 = mod[:, 3], mod[:, 4], mod[:, 5]

        y = rmsnorm(h, norm1_g[l]) * (1.0 + scale1) + shift1
        proj = y @ w_in[l]
        u_ssm, u_pool, g_ssm, g_pool = jnp.split(proj, split_at, axis=-1)
        o_ssm = s5_mixer(u_ssm, ssm_a_re[l], ssm_a_im[l], ssm_log_dt[l], ssm_b_re[l], ssm_b_im[l],
                         ssm_c_re[l], ssm_c_im[l], ssm_d[l], w_glu[l], b_glu[l])
        o_pool = pool_mixer(u_pool, w_pool[l], b_pool[l], pool_scale[l])
        merged = (jax.nn.sigmoid(g_ssm) * (o_ssm @ w_ssm_out[l])
                  + jax.nn.sigmoid(g_pool) * (o_pool @ w_pool_out[l]))
        h = h + gate1 * (merged @ w_o[l])

        y = rmsnorm(h, norm2_g[l]) * (1.0 + scale2) + shift2
        h = h + gate2 * conv_gated_mlp(y, w_up[l], conv_w[l], conv_b[l], w_down[l])
    return rmsnorm(h, final_g)
```

```python
import functools
import math

import jax
import jax.numpy as jnp
from jax import lax
from jax.experimental import pallas as pl
from jax.experimental.pallas import tpu as pltpu

F32 = jnp.float32
BF16 = jnp.bfloat16

CHUNK = 16
SSM_GROUP = 16
SSM_STATE = 64
SLICE_CH = 128
SLICE_GROUPS = SLICE_CH // SSM_GROUP
SLICE_STATES = SLICE_GROUPS * SSM_STATE
POOL_WINDOWS = (2, 4, 8, 16)
CONV_WIDTH = 3
N_MOD = 6
RMS_EPS = 1e-6
LAM_RE_MAX = -1e-4
V7X_VMEM_LIMIT = 56 * 1024 * 1024


def _params(*sem):
    return pltpu.CompilerParams(dimension_semantics=sem, vmem_limit_bytes=V7X_VMEM_LIMIT)


def _shift_rows(x, sh):
    rows, n = x.shape
    if sh >= rows:
        return jnp.zeros_like(x)
    if sh % 8 == 0:
        return jnp.concatenate([jnp.zeros((sh, n), x.dtype), x[:rows - sh]], axis=0)
    rolled = pltpu.roll(x, sh, 0)
    idx = lax.broadcasted_iota(jnp.int32, x.shape, 0)
    return jnp.where(idx < sh, jnp.zeros_like(x), rolled)


def _cond_kernel(c_ref, w_ref, b_ref, ada_ref, o_ref):
    a = jax.nn.silu(c_ref[...]).astype(BF16)
    z = jnp.dot(a, w_ref[...].astype(BF16), preferred_element_type=F32) + b_ref[...]
    o_ref[...] = z[None] + ada_ref[...]


def _cond(c_pad, w_cond, b_cond, ada_table):
    rows, d = c_pad.shape
    n = w_cond.shape[1]
    depth = ada_table.shape[0]
    tn = min(512, n)
    return pl.pallas_call(
        _cond_kernel,
        out_shape=jax.ShapeDtypeStruct((depth, rows, n), F32),
        grid=(n // tn,),
        in_specs=[
            pl.BlockSpec((rows, d), lambda i: (0, 0)),
            pl.BlockSpec((d, tn), lambda i: (0, i)),
            pl.BlockSpec((1, tn), lambda i: (0, i)),
            pl.BlockSpec((depth, 1, tn), lambda i: (0, 0, i)),
        ],
        out_specs=pl.BlockSpec((depth, rows, tn), lambda i: (0, 0, i)),
        compiler_params=_params("arbitrary"),
    )(c_pad, w_cond, b_cond.reshape(1, n), ada_table.reshape(depth, 1, n))


def _norm_kernel(*refs, modulated):
    if modulated:
        x_ref, g_ref, scale_ref, shift_ref, o_ref = refs
    else:
        x_ref, g_ref, o_ref = refs
    d = x_ref.shape[-1]
    x = x_ref[...].reshape(-1, d)
    y = x * lax.rsqrt(jnp.mean(x * x, axis=-1, keepdims=True) + RMS_EPS) * g_ref[...]
    if modulated:
        y = y * (1.0 + scale_ref[0]) + shift_ref[0]
    o_ref[...] = y.astype(o_ref.dtype).reshape(o_ref.shape)


def _norm(x, g, scale, shift, *, in_chunk_major, out_chunk_major, out_dtype):
    if in_chunk_major:
        bsz, nchunk, dd = x.shape
        d = dd // CHUNK
    else:
        bsz, _, nchunk, d = x.shape
    tc = min(256, nchunk)
    cm_spec = pl.BlockSpec((1, tc, d), lambda b, j, c: (b, c, j))
    pm_spec = pl.BlockSpec((1, 1, tc, d), lambda b, j, c: (b, j, c, 0))
    modulated = scale is not None
    in_specs = [cm_spec if in_chunk_major else pm_spec, pl.BlockSpec((1, d), lambda b, j, c: (0, 0))]
    args = [x, g.reshape(1, d)]
    if modulated:
        mod_spec = pl.BlockSpec((1, 1, d), lambda b, j, c: (b, 0, 0))
        in_specs += [mod_spec, mod_spec]
        args += [scale, shift]
    out_shape = (bsz, nchunk, CHUNK * d) if out_chunk_major else (bsz, CHUNK, nchunk, d)
    return pl.pallas_call(
        functools.partial(_norm_kernel, modulated=modulated),
        out_shape=jax.ShapeDtypeStruct(out_shape, out_dtype),
        grid=(bsz, CHUNK, nchunk // tc),
        in_specs=in_specs,
        out_specs=cm_spec if out_chunk_major else pm_spec,
        compiler_params=_params("parallel", "parallel", "parallel"),
    )(*args)


def _proj_kernel(a_ref, w_ref, o_ref, *, first_gate_block):
    z = jnp.dot(a_ref[...], w_ref[...], preferred_element_type=F32)
    n = pl.program_id(0)

    @pl.when(n < first_gate_block)
    def _():
        o_ref[...] = z.astype(o_ref.dtype)

    @pl.when(n >= first_gate_block)
    def _():
        o_ref[...] = jax.nn.sigmoid(z).astype(o_ref.dtype)


def _proj(a, w, n_plain):
    rows, k = a.shape
    n = w.shape[1]
    tm, tn = min(1024, rows), min(1024, n_plain)
    return pl.pallas_call(
        functools.partial(_proj_kernel, first_gate_block=n_plain // tn),
        out_shape=jax.ShapeDtypeStruct((rows, n), BF16),
        grid=(n // tn, rows // tm),
        in_specs=[pl.BlockSpec((tm, k), lambda j, i: (i, 0)), pl.BlockSpec((k, tn), lambda j, i: (0, j))],
        out_specs=pl.BlockSpec((tm, tn), lambda j, i: (i, j)),
        compiler_params=_params("parallel", "parallel"),
    )(a, w)


def _discretise(a_re, a_im, log_dt):
    lam_re = jnp.minimum(a_re, LAM_RE_MAX)
    lam_im = a_im
    dt = jnp.exp(log_dt)
    mag = jnp.exp(lam_re * dt)
    abar_re = mag * jnp.cos(lam_im * dt)
    abar_im = mag * jnp.sin(lam_im * dt)
    den = lam_re * lam_re + lam_im * lam_im
    x_re = abar_re - 1.0
    f_re = (x_re * lam_re + abar_im * lam_im) / den
    f_im = (abar_im * lam_re - x_re * lam_im) / den
    return abar_re, abar_im, f_re, f_im


def _s5_prep_kernel(rowp_ref, colp_ref, bt_ref, ct_ref, kt_ref, p_ref, q_ref, dre_ref, dim_ref):
    ns, ch = SLICE_STATES, SLICE_CH
    ar, ai, f_re, f_im = _discretise(rowp_ref[0, 0, 0:1, :], rowp_ref[0, 0, 1:2, :], rowp_ref[0, 0, 2:3, :])
    bt_re = jnp.concatenate([bt_ref[0, 0, 0]] * SLICE_GROUPS, axis=0)
    bt_im = jnp.concatenate([bt_ref[0, 0, 1]] * SLICE_GROUPS, axis=0)
    same_group = (lax.broadcasted_iota(jnp.int32, (ch, ns), 0) // SSM_GROUP
                  == lax.broadcasted_iota(jnp.int32, (ch, ns), 1) // SSM_STATE)
    e_re = jnp.where(same_group, f_re * bt_re - f_im * bt_im, 0.0)
    e_im = jnp.where(same_group, f_re * bt_im + f_im * bt_re, 0.0)
    same_group_t = (lax.broadcasted_iota(jnp.int32, (ns, ch), 0) // SSM_STATE
                    == lax.broadcasted_iota(jnp.int32, (ns, ch), 1) // SSM_GROUP)
    c_re = jnp.where(same_group_t, ct_ref[0, 0, 0], 0.0)
    c_im = jnp.where(same_group_t, ct_ref[0, 0, 1], 0.0)
    hi = lax.Precision.HIGHEST
    for t in range(CHUNK):
        j = CHUNK - 1 - t
        p_ref[0, 0, j * ch:(j + 1) * ch, 0:ns] = e_re.astype(BF16)
        p_ref[0, 0, j * ch:(j + 1) * ch, ns:2 * ns] = e_im.astype(BF16)
        kt = (jnp.dot(e_re, c_re, precision=hi, preferred_element_type=F32)
              - jnp.dot(e_im, c_im, precision=hi, preferred_element_type=F32))
        kt_ref[0, 0, t] = kt.astype(BF16)
        e_re, e_im = e_re * ar - e_im * ai, e_re * ai + e_im * ar

    d_re, d_im = ar, ai
    for _ in range(int(math.log2(CHUNK))):
        d_re, d_im = d_re * d_re - d_im * d_im, 2.0 * d_re * d_im
    for k in range(dre_ref.shape[2]):
        dre_ref[0, 0, k:k + 1, :] = d_re
        dim_ref[0, 0, k:k + 1, :] = d_im
        d_re, d_im = d_re * d_re - d_im * d_im, 2.0 * d_re * d_im

    arc, aic, _, _ = _discretise(colp_ref[0, 0, 0], colp_ref[0, 0, 1], colp_ref[0, 0, 2])
    w_re, w_im = c_re, c_im
    for i in range(CHUNK):
        w_re, w_im = w_re * arc - w_im * aic, w_re * aic + w_im * arc
        q_ref[0, 0, 0:ns, i * ch:(i + 1) * ch] = w_re.astype(BF16)
        q_ref[0, 0, ns:2 * ns, i * ch:(i + 1) * ch] = (-w_im).astype(BF16)


def _s5_prep(a_re, a_im, log_dt, b_re, b_im, c_re, c_im):
    depth, groups, nst = a_re.shape
    nsl = groups // SLICE_GROUPS
    ns, ch = SLICE_STATES, SLICE_CH
    dt_b = jnp.broadcast_to(log_dt[:, :, None], a_re.shape)
    rows = jnp.stack([a_re, a_im, dt_b], axis=2).reshape(depth, nsl, SLICE_GROUPS, 3, nst)
    rows = jnp.swapaxes(rows, 2, 3).reshape(depth, nsl, 3, ns)
    rowp = jnp.pad(rows, ((0, 0), (0, 0), (0, 5), (0, 0)))
    colp = jnp.broadcast_to(rows[..., None], (depth, nsl, 3, ns, ch))
    bt = jnp.stack([b_re, b_im], axis=1).reshape(depth, 2, nsl, ns, SSM_GROUP)
    bt = jnp.transpose(bt, (0, 2, 1, 4, 3))
    ct = jnp.stack([c_re, c_im], axis=1).reshape(depth, 2, nsl, SLICE_GROUPS, SSM_GROUP, nst)
    ct = jnp.transpose(ct, (0, 2, 1, 3, 5, 4)).reshape(depth, nsl, 2, ns, SSM_GROUP)
    ct = jnp.tile(ct, (1, 1, 1, 1, SLICE_GROUPS))
    nk = 16
    out_shapes = (
        jax.ShapeDtypeStruct((depth, nsl, CHUNK, ch, ch), BF16),
        jax.ShapeDtypeStruct((depth, nsl, CHUNK * ch, 2 * ns), BF16),
        jax.ShapeDtypeStruct((depth, nsl, 2 * ns, CHUNK * ch), BF16),
        jax.ShapeDtypeStruct((depth, nsl, nk, ns), F32),
        jax.ShapeDtypeStruct((depth, nsl, nk, ns), F32),
    )
    return pl.pallas_call(
        _s5_prep_kernel,
        out_shape=out_shapes,
        grid=(depth, nsl),
        in_specs=[
            pl.BlockSpec((1, 1, 8, ns), lambda l, s: (l, s, 0, 0)),
            pl.BlockSpec((1, 1, 3, ns, ch), lambda l, s: (l, s, 0, 0, 0)),
            pl.BlockSpec((1, 1, 2, SSM_GROUP, ns), lambda l, s: (l, s, 0, 0, 0)),
            pl.BlockSpec((1, 1, 2, ns, ch), lambda l, s: (l, s, 0, 0, 0)),
        ],
        out_specs=(
            pl.BlockSpec((1, 1, CHUNK, ch, ch), lambda l, s: (l, s, 0, 0, 0)),
            pl.BlockSpec((1, 1, CHUNK * ch, 2 * ns), lambda l, s: (l, s, 0, 0)),
            pl.BlockSpec((1, 1, 2 * ns, CHUNK * ch), lambda l, s: (l, s, 0, 0)),
            pl.BlockSpec((1, 1, nk, ns), lambda l, s: (l, s, 0, 0)),
            pl.BlockSpec((1, 1, nk, ns), lambda l, s: (l, s, 0, 0)),
        ),
        compiler_params=_params("parallel", "parallel"),
    )(rowp, colp, bt, ct)


def _s5_kernel(u_ref, kt_ref, p_ref, q_ref, dre_ref, dim_ref, dskip_ref, o_ref, toep_ref):
    ch, ns = SLICE_CH, SLICE_STATES
    nchunk = u_ref.shape[2]

    @pl.when(pl.program_id(1) == 0)
    def _():
        toep_ref[...] = jnp.zeros(toep_ref.shape, toep_ref.dtype)
        for i in range(CHUNK):
            for j in range(i + 1):
                toep_ref[j * ch:(j + 1) * ch, i * ch:(i + 1) * ch] = kt_ref[0, 0, i - j]

    u_all = jnp.concatenate([u_ref[0, j] for j in range(CHUNK)], axis=-1)
    x_loc = jnp.dot(u_all, p_ref[0, 0], preferred_element_type=F32)
    x_re, x_im = x_loc[:, :ns], x_loc[:, ns:]
    k, sh = 0, 1
    while sh < nchunk:
        d_re, d_im = dre_ref[0, 0, k:k + 1, :], dim_ref[0, 0, k:k + 1, :]
        s_re, s_im = _shift_rows(x_re, sh), _shift_rows(x_im, sh)
        x_re, x_im = x_re + d_re * s_re - d_im * s_im, x_im + d_re * s_im + d_im * s_re
        k, sh = k + 1, sh * 2
    x_prev = jnp.concatenate([_shift_rows(x_re, 1), _shift_rows(x_im, 1)], axis=-1).astype(BF16)
    y_all = (jnp.dot(u_all, toep_ref[...], preferred_element_type=F32)
             + jnp.dot(x_prev, q_ref[0, 0], preferred_element_type=F32))
    for i in range(CHUNK):
        y = y_all[:, i * ch:(i + 1) * ch] + u_ref[0, i].astype(F32) * dskip_ref[...]
        o_ref[0, i] = jax.nn.gelu(y).astype(o_ref.dtype)


def _s5(proj4, kt, p, q, dre, dim, d_skip, layer):
    bsz, _, nchunk, _ = proj4.shape
    ch, ns = SLICE_CH, SLICE_STATES
    nsl = kt.shape[1]
    assert 2 ** dre.shape[2] >= nchunk
    return pl.pallas_call(
        _s5_kernel,
        out_shape=jax.ShapeDtypeStruct((bsz, CHUNK, nchunk, nsl * ch), BF16),
        grid=(nsl, bsz),
        in_specs=[
            pl.BlockSpec((1, CHUNK, nchunk, ch), lambda s, b: (b, 0, 0, s)),
            pl.BlockSpec((1, 1, CHUNK, ch, ch), lambda s, b: (layer, s, 0, 0, 0)),
            pl.BlockSpec((1, 1, CHUNK * ch, 2 * ns), lambda s, b: (layer, s, 0, 0)),
            pl.BlockSpec((1, 1, 2 * ns, CHUNK * ch), lambda s, b: (layer, s, 0, 0)),
            pl.BlockSpec((1, 1, dre.shape[2], ns), lambda s, b: (layer, s, 0, 0)),
            pl.BlockSpec((1, 1, dim.shape[2], ns), lambda s, b: (layer, s, 0, 0)),
            pl.BlockSpec((1, ch), lambda s, b: (0, s)),
        ],
        out_specs=pl.BlockSpec((1, CHUNK, nchunk, ch), lambda s, b: (b, 0, 0, s)),
        scratch_shapes=[pltpu.VMEM((CHUNK * ch, CHUNK * ch), BF16)],
        compiler_params=_params("arbitrary", "arbitrary"),
    )(proj4, kt, p, q, dre, dim, d_skip.reshape(1, -1))


def _glu_kernel(a_ref, w_ref, b_ref, y_ref, o_ref):
    z = jnp.dot(a_ref[...], w_ref[...], preferred_element_type=F32) + b_ref[...]
    o_ref[...] = (y_ref[...].astype(F32) * jax.nn.sigmoid(z)).astype(o_ref.dtype)


def _glu(y, w, b):
    rows, k = y.shape
    n = w.shape[1]
    tm, tn = min(1024, rows), min(1024, n)
    return pl.pallas_call(
        _glu_kernel,
        out_shape=jax.ShapeDtypeStruct((rows, n), BF16),
        grid=(n // tn, rows // tm),
        in_specs=[
            pl.BlockSpec((tm, k), lambda j, i: (i, 0)),
            pl.BlockSpec((k, tn), lambda j, i: (0, j)),
            pl.BlockSpec((1, tn), lambda j, i: (0, j)),
            pl.BlockSpec((tm, tn), lambda j, i: (i, j)),
        ],
        out_specs=pl.BlockSpec((tm, tn), lambda j, i: (i, j)),
        compiler_params=_params("parallel", "parallel"),
    )(y, w, b.reshape(1, n), y)


def _pool_kernel(v_ref, o_ref, *, blocks_per_group):
    nchunk, tn = v_ref.shape[2], v_ref.shape[3]
    group = pl.program_id(1) // blocks_per_group
    first_row = lax.broadcasted_iota(jnp.int32, (nchunk, 1), 0) == 0

    def pooled(win):
        v = [v_ref[0, j].astype(F32) for j in range(CHUNK)]
        prefix = [v[0]]
        for j in range(1, CHUNK):
            prefix.append(prefix[-1] + v[j])
        for j in range(CHUNK):
            if j >= win:
                wsum = prefix[j] - prefix[j - win]
            elif j == win - 1:
                wsum = prefix[j]
            else:
                wsum = prefix[j] + _shift_rows(prefix[CHUNK - 1] - prefix[j - win + CHUNK], 1)
            count = jnp.where(first_row, float(min(j + 1, win)), float(win))
            o_ref[0, j] = (wsum / count - v[j]).astype(o_ref.dtype)

    for g, win in enumerate(POOL_WINDOWS):
        pl.when(group == g)(functools.partial(pooled, win))


def _pool(proj4, col0, width):
    bsz, _, nchunk, _ = proj4.shape
    tn = 128
    blocks_per_group = width // len(POOL_WINDOWS) // tn
    return pl.pallas_call(
        functools.partial(_pool_kernel, blocks_per_group=blocks_per_group),
        out_shape=jax.ShapeDtypeStruct((bsz, CHUNK, nchunk, width), BF16),
        grid=(bsz, width // tn),
        in_specs=[pl.BlockSpec((1, CHUNK, nchunk, tn), lambda b, t: (b, 0, 0, col0 // tn + t))],
        out_specs=pl.BlockSpec((1, CHUNK, nchunk, tn), lambda b, t: (b, 0, 0, t)),
        compiler_params=_params("parallel", "parallel"),
    )(proj4)


def _pool_mix_kernel(a_ref, w_ref, b_ref, s_ref, o_ref):
    z = jnp.dot(a_ref[...], w_ref[0], preferred_element_type=F32)
    o_ref[...] = ((z + b_ref[0]) * s_ref[...]).astype(o_ref.dtype)


def _pool_mix(pooled, w, b, scale):
    rows, width = pooled.shape
    ngroups, gw, _ = w.shape
    tm = min(2048, rows)
    return pl.pallas_call(
        _pool_mix_kernel,
        out_shape=jax.ShapeDtypeStruct((rows, width), BF16),
        grid=(ngroups, rows // tm),
        in_specs=[
            pl.BlockSpec((tm, gw), lambda g, i: (i, g)),
            pl.BlockSpec((1, gw, gw), lambda g, i: (g, 0, 0)),
            pl.BlockSpec((1, 1, gw), lambda g, i: (g, 0, 0)),
            pl.BlockSpec((1, gw), lambda g, i: (0, g)),
        ],
        out_specs=pl.BlockSpec((tm, gw), lambda g, i: (i, g)),
        compiler_params=_params("parallel", "parallel"),
    )(pooled, w, b.reshape(ngroups, 1, gw), scale.reshape(1, width))


def _merge_kernel(a1_ref, w1_ref, a2_ref, w2_ref, g1_ref, g2_ref, o_ref):
    z1 = jnp.dot(a1_ref[...], w1_ref[...], preferred_element_type=F32)
    z2 = jnp.dot(a2_ref[...], w2_ref[...], preferred_element_type=F32)
    o_ref[...] = (g1_ref[...].astype(F32) * z1 + g2_ref[...].astype(F32) * z2).astype(o_ref.dtype)


def _merge(o_ssm, w_ssm_out, o_pool, w_pool_out, proj, gate_col0):
    rows, k1 = o_ssm.shape
    k2 = o_pool.shape[1]
    n = w_ssm_out.shape[1]
    tm, tn = min(1024, rows), min(512, n)
    g1_blk, g2_blk = gate_col0 // tn, (gate_col0 + n) // tn
    return pl.pallas_call(
        _merge_kernel,
        out_shape=jax.ShapeDtypeStruct((rows, n), BF16),
        grid=(n // tn, rows // tm),
        in_specs=[
            pl.BlockSpec((tm, k1), lambda j, i: (i, 0)),
            pl.BlockSpec((k1, tn), lambda j, i: (0, j)),
            pl.BlockSpec((tm, k2), lambda j, i: (i, 0)),
            pl.BlockSpec((k2, tn), lambda j, i: (0, j)),
            pl.BlockSpec((tm, tn), lambda j, i: (i, g1_blk + j)),
            pl.BlockSpec((tm, tn), lambda j, i: (i, g2_blk + j)),
        ],
        out_specs=pl.BlockSpec((tm, tn), lambda j, i: (i, j)),
        compiler_params=_params("parallel", "parallel"),
    )(o_ssm, w_ssm_out, o_pool, w_pool_out, proj, proj)


def _residual_kernel(a_ref, w_ref, h_ref, g_ref, o_ref):
    z = jnp.dot(a_ref[...], w_ref[...], preferred_element_type=F32)
    o_ref[...] = h_ref[...].reshape(o_ref.shape) + g_ref[0] * z


def _residual(a, w, h, gate, *, tm, tn, h_chunk_major=False):
    rows, k = a.shape
    n = w.shape[1]
    rows_per_seq = rows // gate.shape[0]
    nchunk = rows_per_seq // CHUNK
    tm, tn = min(tm, nchunk if h_chunk_major else rows_per_seq), min(tn, n)
    if h_chunk_major:
        h_spec = pl.BlockSpec((1, tm, tn), lambda j, i: (i // CHUNK, 0, (i % CHUNK) * (n // tn) + j))
    else:
        h_spec = pl.BlockSpec((tm, tn), lambda j, i: (i, j))
    return pl.pallas_call(
        _residual_kernel,
        out_shape=jax.ShapeDtypeStruct((rows, n), F32),
        grid=(n // tn, rows // tm),
        in_specs=[
            pl.BlockSpec((tm, k), lambda j, i: (i, 0)),
            pl.BlockSpec((k, tn), lambda j, i: (0, j)),
            h_spec,
            pl.BlockSpec((1, 1, tn), lambda j, i: (i * tm // rows_per_seq, 0, j)),
        ],
        out_specs=pl.BlockSpec((tm, tn), lambda j, i: (i, j)),
        input_output_aliases={} if h_chunk_major else {2: 0},
        compiler_params=_params("parallel", "parallel"),
    )(a, w, h, gate)


def _up_conv_kernel(a_ref, wg_ref, wv_ref, cwg_ref, cwv_ref, cbg_ref, cbv_ref, o_ref, carry_ref):
    _, _, cm, d = a_ref.shape
    tn = o_ref.shape[-1]

    @pl.when(pl.program_id(2) == 0)
    def _():
        carry_ref[...] = jnp.zeros(carry_ref.shape, carry_ref.dtype)

    a = a_ref[0].reshape(CHUNK * cm, d)
    first_row = lax.broadcasted_iota(jnp.int32, (cm, tn), 0) == 0

    def conv(w_ref, cw_ref, cb_ref, slot):
        r = jnp.dot(a, w_ref[...], preferred_element_type=F32).reshape(CHUNK, cm, tn)
        prev = []
        for t in range(CONV_WIDTH - 1):
            plane = r[CHUNK - (CONV_WIDTH - 1) + t]
            prev.append(jnp.where(first_row, carry_ref[slot + t, 0:1, :], pltpu.roll(plane, 1, 0)))
            carry_ref[slot + t] = jnp.broadcast_to(plane[cm - 1:cm, :], (8, tn))
        planes = prev + [r[j] for j in range(CHUNK)]
        outs = []
        for j in range(CHUNK):
            acc = cb_ref[...]
            for t in range(CONV_WIDTH):
                acc = acc + cw_ref[t:t + 1, :] * planes[j + t]
            outs.append(acc)
        return outs

    gate = conv(wg_ref, cwg_ref, cbg_ref, 0)
    val = conv(wv_ref, cwv_ref, cbv_ref, CONV_WIDTH - 1)
    for j in range(CHUNK):
        o_ref[0, j] = (jax.nn.silu(gate[j]) * val[j]).astype(o_ref.dtype)


def _up_conv(y4, w_up, conv_w, conv_b):
    bsz, _, nchunk, d = y4.shape
    d_ff = w_up.shape[1] // 2
    cm, tn = min(32, nchunk), min(512, d_ff)
    nblk = d_ff // tn
    conv_b = conv_b.reshape(1, -1)
    return pl.pallas_call(
        _up_conv_kernel,
        out_shape=jax.ShapeDtypeStruct((bsz, CHUNK, nchunk, d_ff), BF16),
        grid=(nblk, bsz, nchunk // cm),
        in_specs=[
            pl.BlockSpec((1, CHUNK, cm, d), lambda n, b, c: (b, 0, c, 0)),
            pl.BlockSpec((d, tn), lambda n, b, c: (0, n)),
            pl.BlockSpec((d, tn), lambda n, b, c: (0, nblk + n)),
            pl.BlockSpec((CONV_WIDTH, tn), lambda n, b, c: (0, n)),
            pl.BlockSpec((CONV_WIDTH, tn), lambda n, b, c: (0, nblk + n)),
            pl.BlockSpec((1, tn), lambda n, b, c: (0, n)),
            pl.BlockSpec((1, tn), lambda n, b, c: (0, nblk + n)),
        ],
        out_specs=pl.BlockSpec((1, CHUNK, cm, tn), lambda n, b, c: (b, 0, c, n)),
        scratch_shapes=[pltpu.VMEM((2 * (CONV_WIDTH - 1), 8, tn), F32)],
        compiler_params=_params("arbitrary", "arbitrary", "arbitrary"),
    )(y4, w_up, w_up, conv_w, conv_w, conv_b, conv_b)


def kernel(x, c, w_cond, b_cond, ada_table, norm1_g, norm2_g, w_in, ssm_a_re, ssm_a_im, ssm_log_dt, ssm_b_re, ssm_b_im, ssm_c_re, ssm_c_im, ssm_d, w_glu, b_glu, w_pool, b_pool, pool_scale, w_ssm_out, w_pool_out, w_o, w_up, conv_w, conv_b, w_down, final_g):
    bsz, seq, d = x.shape
    depth = w_in.shape[0]
    nchunk = seq // CHUNK
    rows = bsz * seq
    ssm_width = w_glu.shape[1]
    pool_width = pool_scale.shape[1]
    assert seq % CHUNK == 0 and ssm_width % SLICE_CH == 0

    c_pad = jnp.pad(c, ((0, 8 - bsz), (0, 0)))
    mods = _cond(c_pad, w_cond, b_cond, ada_table)[:, :bsz].reshape(depth, bsz, N_MOD, 1, d)
    kt, p, q, dre, dim = _s5_prep(ssm_a_re, ssm_a_im, ssm_log_dt, ssm_b_re, ssm_b_im, ssm_c_re, ssm_c_im)

    h = None
    for l in range(depth):
        shift1, scale1, gate1, shift2, scale2, gate2 = (mods[l, :, i] for i in range(N_MOD))
        if l == 0:
            y = _norm(x.reshape(bsz, nchunk, CHUNK * d), norm1_g[l], scale1, shift1,
                      in_chunk_major=True, out_chunk_major=False, out_dtype=BF16)
        else:
            y = _norm(h.reshape(bsz, CHUNK, nchunk, d), norm1_g[l], scale1, shift1,
                      in_chunk_major=False, out_chunk_major=False, out_dtype=BF16)
        proj = _proj(y.reshape(rows, d), w_in[l].astype(BF16), ssm_width + pool_width)
        proj4 = proj.reshape(bsz, CHUNK, nchunk, -1)
        y_ssm = _s5(proj4, kt, p, q, dre, dim, ssm_d[l], l).reshape(rows, ssm_width)
        o_ssm = _glu(y_ssm, w_glu[l].astype(BF16), b_glu[l])
        pooled = _pool(proj4, ssm_width, pool_width).reshape(rows, pool_width)
        o_pool = _pool_mix(pooled, w_pool[l].astype(BF16), b_pool[l], pool_scale[l])
        merged = _merge(o_ssm, w_ssm_out[l].astype(BF16), o_pool, w_pool_out[l].astype(BF16),
                        proj, ssm_width + pool_width)
        if l == 0:
            h = _residual(merged, w_o[l].astype(BF16), x.reshape(bsz, nchunk, CHUNK * d), gate1,
                          tm=1024, tn=1024, h_chunk_major=True)
        else:
            h = _residual(merged, w_o[l].astype(BF16), h, gate1, tm=1024, tn=1024)

        y = _norm(h.reshape(bsz, CHUNK, nchunk, d), norm2_g[l], scale2, shift2,
                  in_chunk_major=False, out_chunk_major=False, out_dtype=BF16)
        act = _up_conv(y, w_up[l].astype(BF16), conv_w[l], conv_b[l]).reshape(rows, -1)
        h = _residual(act, w_down[l].astype(BF16), h, gate2, tm=512, tn=512)

    out = _norm(h.reshape(bsz, CHUNK, nchunk, d), final_g, None, None,
                in_chunk_major=False, out_chunk_major=True, out_dtype=x.dtype)
    return out.reshape(bsz, seq, d)
```

```python
import functools

import jax
import jax.numpy as jnp
from jax import lax
from jax.experimental import pallas as pl
from jax.experimental.pallas import tpu as pltpu

F32 = jnp.float32
BF16 = jnp.bfloat16

CHUNK = 16
SSM_GROUP = 16
SSM_STATE = 64
SLICE_CH = 128
SLICE_GROUPS = SLICE_CH // SSM_GROUP
SLICE_STATES = SLICE_GROUPS * SSM_STATE
POOL_WINDOWS = (2, 4, 8, 16)
CONV_WIDTH = 3
N_MOD = 6
RMS_EPS = 1e-6
LAM_RE_MAX = -1e-4
SUBLANES = 8
LANES = 128
MXU_COLS = 256
V7X_VMEM_LIMIT = 56 * 1024 * 1024
ROW_TILE = 1024
COL_TILE = 512


def _params(*sem):
    return pltpu.CompilerParams(dimension_semantics=sem, vmem_limit_bytes=V7X_VMEM_LIMIT)


def _shift_rows(x, sh):
    rolled = pltpu.roll(x, sh, 0)
    idx = lax.broadcasted_iota(jnp.int32, x.shape, 0)
    return jnp.where(idx < sh, jnp.zeros_like(x), rolled)


def _panel(w_ref, wbf_ref, first):
    if wbf_ref is None:
        return w_ref.at[0]

    @pl.when(first)
    def _():
        wbf_ref[...] = w_ref[0].astype(BF16)

    return wbf_ref


def _for_col_blocks(width, body):
    sub = min(MXU_COLS, width)
    for j in range(width // sub):
        body(slice(j * sub, (j + 1) * sub))


def _weight_specs(w, layer, tn, col_block):
    k = w.shape[1]
    spec = pl.BlockSpec((1, k, tn), lambda j, i: (layer, 0, col_block(j)))
    scratch = [] if w.dtype == BF16 else [pltpu.VMEM((k, tn), BF16)]
    return spec, scratch


def _cond_kernel(c_ref, w_ref, b_ref, ada_ref, o_ref):
    a = jax.nn.silu(c_ref[...]).astype(BF16)
    z = jnp.dot(a, w_ref[...].astype(BF16), preferred_element_type=F32) + b_ref[...]
    o_ref[...] = z[None] + ada_ref[...]


def _cond(c_pad, w_cond, b_cond, ada_table):
    rows, d = c_pad.shape
    n = w_cond.shape[1]
    depth = ada_table.shape[0]
    tn = min(COL_TILE, n)
    return pl.pallas_call(
        _cond_kernel,
        out_shape=jax.ShapeDtypeStruct((depth, rows, n), F32),
        grid=(n // tn,),
        in_specs=[
            pl.BlockSpec((rows, d), lambda i: (0, 0)),
            pl.BlockSpec((d, tn), lambda i: (0, i)),
            pl.BlockSpec((1, tn), lambda i: (0, i)),
            pl.BlockSpec((depth, 1, tn), lambda i: (0, 0, i)),
        ],
        out_specs=pl.BlockSpec((depth, rows, tn), lambda i: (0, 0, i)),
        compiler_params=_params("arbitrary"),
    )(c_pad, w_cond, b_cond.reshape(1, n), ada_table.reshape(depth, 1, n))


def _norm_kernel(*refs, modulated, in_tokens, out_tokens, copy_out):
    refs = list(refs)
    x_ref, g_ref = refs[:2]
    scale_ref, shift_ref = refs[2:4] if modulated else (None, None)
    outs = refs[4 if modulated else 2:]
    o_ref = outs[0]
    copy_ref = outs[1] if copy_out else None
    stage_ref = outs[-1] if (in_tokens or out_tokens) else None
    tc = (x_ref.shape[1] // CHUNK) if in_tokens else x_ref.shape[2]
    lane_tiles = x_ref.shape[-1] // LANES
    plane_rows = lambda j: pl.ds(j, tc, stride=CHUNK)
    if in_tokens:
        for k in range(lane_tiles):
            stage_ref[k] = x_ref[0, :, k * LANES:(k + 1) * LANES]
    for j in range(CHUNK):
        if in_tokens:
            x = jnp.concatenate([stage_ref[k, plane_rows(j), :] for k in range(lane_tiles)], axis=-1)
        else:
            x = x_ref[0, j]
        y = x * lax.rsqrt(jnp.mean(x * x, axis=-1, keepdims=True) + RMS_EPS) * g_ref[...]
        if modulated:
            y = y * (1.0 + scale_ref[0]) + shift_ref[0]
        if out_tokens:
            for k in range(lane_tiles):
                stage_ref[k, plane_rows(j), :] = y[:, k * LANES:(k + 1) * LANES]
        else:
            o_ref[0, j] = y.astype(o_ref.dtype)
        if copy_out:
            copy_ref[0, j] = x
    if out_tokens:
        for k in range(lane_tiles):
            o_ref[0, :, k * LANES:(k + 1) * LANES] = stage_ref[k].astype(o_ref.dtype)


def _norm(x, g, scale, shift, *, in_tokens=False, out_tokens=False, copy_out=False, out_dtype=BF16):
    if in_tokens:
        bsz, seq, d = x.shape
        nchunk = seq // CHUNK
    else:
        bsz, _, nchunk, d = x.shape
    tc = min(16, nchunk)
    tok_spec = pl.BlockSpec((1, tc * CHUNK, d), lambda b, c: (b, c, 0))
    pos_spec = pl.BlockSpec((1, CHUNK, tc, d), lambda b, c: (b, 0, c, 0))
    modulated = scale is not None
    in_specs = [tok_spec if in_tokens else pos_spec, pl.BlockSpec((1, d), lambda b, c: (0, 0))]
    args = [x, g.reshape(1, d)]
    if modulated:
        mod_spec = pl.BlockSpec((1, 1, d), lambda b, c: (b, 0, 0))
        in_specs += [mod_spec, mod_spec]
        args += [scale, shift]
    pos_shape = (bsz, CHUNK, nchunk, d)
    out_shape = [jax.ShapeDtypeStruct((bsz, nchunk * CHUNK, d) if out_tokens else pos_shape, out_dtype)]
    out_specs = [tok_spec if out_tokens else pos_spec]
    if copy_out:
        out_shape.append(jax.ShapeDtypeStruct(pos_shape, F32))
        out_specs.append(pos_spec)
    staged = in_tokens or out_tokens
    scratch = [pltpu.VMEM((d // LANES, tc * CHUNK, LANES), F32)] if staged else []
    outs = pl.pallas_call(
        functools.partial(_norm_kernel, modulated=modulated, in_tokens=in_tokens, out_tokens=out_tokens,
                          copy_out=copy_out),
        out_shape=out_shape,
        grid=(bsz, nchunk // tc),
        in_specs=in_specs,
        out_specs=out_specs,
        scratch_shapes=scratch,
        compiler_params=_params("parallel", "parallel"),
    )(*args)
    return outs if copy_out else outs[0]


def _proj_kernel(a_ref, w_ref, o_ref, *scratch, sigmoid):
    w = _panel(w_ref, scratch[0] if scratch else None, pl.program_id(1) == 0)

    def block(cols):
        z = jnp.dot(a_ref[...], w[:, cols], preferred_element_type=F32)
        o_ref[:, cols] = (jax.nn.sigmoid(z) if sigmoid else z).astype(o_ref.dtype)

    _for_col_blocks(o_ref.shape[1], block)


def _proj(a, w, layer, col0, n, *, sigmoid):
    rows, k = a.shape
    tm, tn = min(ROW_TILE, rows), min(COL_TILE, n)
    w_spec, scratch = _weight_specs(w, layer, tn, lambda j: col0 // tn + j)
    return pl.pallas_call(
        functools.partial(_proj_kernel, sigmoid=sigmoid),
        out_shape=jax.ShapeDtypeStruct((rows, n), BF16),
        grid=(n // tn, rows // tm),
        in_specs=[pl.BlockSpec((tm, k), lambda j, i: (i, 0)), w_spec],
        out_specs=pl.BlockSpec((tm, tn), lambda j, i: (i, j)),
        scratch_shapes=scratch,
        compiler_params=_params("arbitrary", "arbitrary"),
    )(a, w)


def _discretise(a_re, a_im, log_dt):
    lam_re = jnp.minimum(a_re, LAM_RE_MAX)
    lam_im = a_im
    dt = jnp.exp(log_dt)
    mag = jnp.exp(lam_re * dt)
    abar_re = mag * jnp.cos(lam_im * dt)
    abar_im = mag * jnp.sin(lam_im * dt)
    den = lam_re * lam_re + lam_im * lam_im
    x_re = abar_re - 1.0
    f_re = (x_re * lam_re + abar_im * lam_im) / den
    f_im = (abar_im * lam_re - x_re * lam_im) / den
    return abar_re, abar_im, f_re, f_im


def _split_bf16(x):
    hi = x.astype(BF16)
    return hi, (x - hi.astype(F32)).astype(BF16)


def _dot_split(a, b):
    a_hi, a_lo = a
    b_hi, b_lo = b
    dot = functools.partial(jnp.dot, preferred_element_type=F32)
    return dot(a_hi, b_hi) + (dot(a_hi, b_lo) + dot(a_lo, b_hi))


def _s5_prep_kernel(rowp_ref, colp_ref, bt_ref, ct_ref, kt_ref, p_ref, q_ref, dre_ref, dim_ref):
    ns, ch = SLICE_STATES, SLICE_CH
    ar, ai, f_re, f_im = _discretise(rowp_ref[0, 0, 0:1, :], rowp_ref[0, 0, 1:2, :], rowp_ref[0, 0, 2:3, :])
    bt_re = jnp.concatenate([bt_ref[0, 0, 0]] * SLICE_GROUPS, axis=0)
    bt_im = jnp.concatenate([bt_ref[0, 0, 1]] * SLICE_GROUPS, axis=0)
    same_group = (lax.broadcasted_iota(jnp.int32, (ch, ns), 0) // SSM_GROUP
                  == lax.broadcasted_iota(jnp.int32, (ch, ns), 1) // SSM_STATE)
    e_re = jnp.where(same_group, f_re * bt_re - f_im * bt_im, 0.0)
    e_im = jnp.where(same_group, f_re * bt_im + f_im * bt_re, 0.0)
    same_group_t = (lax.broadcasted_iota(jnp.int32, (ns, ch), 0) // SSM_STATE
                    == lax.broadcasted_iota(jnp.int32, (ns, ch), 1) // SSM_GROUP)
    c_re = jnp.where(same_group_t, ct_ref[0, 0, 0], 0.0)
    c_im = jnp.where(same_group_t, ct_ref[0, 0, 1], 0.0)
    c_re_split, c_im_split = _split_bf16(c_re), _split_bf16(c_im)
    for t in range(CHUNK):
        j = CHUNK - 1 - t
        e_re_split, e_im_split = _split_bf16(e_re), _split_bf16(e_im)
        p_ref[0, 0, j * ch:(j + 1) * ch, 0:ns] = e_re_split[0]
        p_ref[0, 0, j * ch:(j + 1) * ch, ns:2 * ns] = e_im_split[0]
        kt_ref[0, 0, t] = (_dot_split(e_re_split, c_re_split) - _dot_split(e_im_split, c_im_split)).astype(BF16)
        e_re, e_im = e_re * ar - e_im * ai, e_re * ai + e_im * ar

    d_re, d_im = ar, ai
    for _ in range(CHUNK.bit_length() - 1):
        d_re, d_im = d_re * d_re - d_im * d_im, 2.0 * d_re * d_im
    p_re, p_im = d_re, d_im
    for r in range(SUBLANES):
        dre_ref[0, 0, r:r + 1, :] = p_re
        dim_ref[0, 0, r:r + 1, :] = p_im
        p_re, p_im = p_re * d_re - p_im * d_im, p_re * d_im + p_im * d_re

    arc, aic, _, _ = _discretise(colp_ref[0, 0, 0], colp_ref[0, 0, 1], colp_ref[0, 0, 2])
    w_re, w_im = c_re, c_im
    for i in range(CHUNK):
        w_re, w_im = w_re * arc - w_im * aic, w_re * aic + w_im * arc
        q_ref[0, 0, 0:ns, i * ch:(i + 1) * ch] = w_re.astype(BF16)
        q_ref[0, 0, ns:2 * ns, i * ch:(i + 1) * ch] = (-w_im).astype(BF16)


def _s5_prep(a_re, a_im, log_dt, b_re, b_im, c_re, c_im):
    depth, groups, nst = a_re.shape
    nsl = groups // SLICE_GROUPS
    ns, ch = SLICE_STATES, SLICE_CH
    dt_b = jnp.broadcast_to(log_dt[:, :, None], a_re.shape)
    rows = jnp.stack([a_re, a_im, dt_b], axis=2).reshape(depth, nsl, SLICE_GROUPS, 3, nst)
    rows = jnp.swapaxes(rows, 2, 3).reshape(depth, nsl, 3, ns)
    rowp = jnp.pad(rows, ((0, 0), (0, 0), (0, SUBLANES - 3), (0, 0)))
    colp = jnp.broadcast_to(rows[..., None], (depth, nsl, 3, ns, ch))
    bt = jnp.stack([b_re, b_im], axis=1).reshape(depth, 2, nsl, ns, SSM_GROUP)
    bt = jnp.transpose(bt, (0, 2, 1, 4, 3))
    ct = jnp.stack([c_re, c_im], axis=1).reshape(depth, 2, nsl, SLICE_GROUPS, SSM_GROUP, nst)
    ct = jnp.transpose(ct, (0, 2, 1, 3, 5, 4)).reshape(depth, nsl, 2, ns, SSM_GROUP)
    ct = jnp.tile(ct, (1, 1, 1, 1, SLICE_GROUPS))
    out_shapes = (
        jax.ShapeDtypeStruct((depth, nsl, CHUNK, ch, ch), BF16),
        jax.ShapeDtypeStruct((depth, nsl, CHUNK * ch, 2 * ns), BF16),
        jax.ShapeDtypeStruct((depth, nsl, 2 * ns, CHUNK * ch), BF16),
        jax.ShapeDtypeStruct((depth, nsl, SUBLANES, ns), F32),
        jax.ShapeDtypeStruct((depth, nsl, SUBLANES, ns), F32),
    )
    return pl.pallas_call(
        _s5_prep_kernel,
        out_shape=out_shapes,
        grid=(depth, nsl),
        in_specs=[
            pl.BlockSpec((1, 1, SUBLANES, ns), lambda l, s: (l, s, 0, 0)),
            pl.BlockSpec((1, 1, 3, ns, ch), lambda l, s: (l, s, 0, 0, 0)),
            pl.BlockSpec((1, 1, 2, SSM_GROUP, ns), lambda l, s: (l, s, 0, 0, 0)),
            pl.BlockSpec((1, 1, 2, ns, ch), lambda l, s: (l, s, 0, 0, 0)),
        ],
        out_specs=(
            pl.BlockSpec((1, 1, CHUNK, ch, ch), lambda l, s: (l, s, 0, 0, 0)),
            pl.BlockSpec((1, 1, CHUNK * ch, 2 * ns), lambda l, s: (l, s, 0, 0)),
            pl.BlockSpec((1, 1, 2 * ns, CHUNK * ch), lambda l, s: (l, s, 0, 0)),
            pl.BlockSpec((1, 1, SUBLANES, ns), lambda l, s: (l, s, 0, 0)),
            pl.BlockSpec((1, 1, SUBLANES, ns), lambda l, s: (l, s, 0, 0)),
        ),
        compiler_params=_params("parallel", "parallel"),
    )(rowp, colp, bt, ct)


def _chunk_scan(x_re, x_im, dre_ref, dim_ref):
    nrows = x_re.shape[0]
    row_in_block = lax.broadcasted_iota(jnp.int32, x_re.shape, 0) % SUBLANES
    sh = 1
    while sh < SUBLANES:
        d_re, d_im = dre_ref[0, 0, sh - 1:sh, :], dim_ref[0, 0, sh - 1:sh, :]
        keep = row_in_block >= sh
        s_re = jnp.where(keep, pltpu.roll(x_re, sh, 0), 0.0)
        s_im = jnp.where(keep, pltpu.roll(x_im, sh, 0), 0.0)
        x_re, x_im = x_re + d_re * s_re - d_im * s_im, x_im + d_re * s_im + d_im * s_re
        sh *= 2
    pw_re, pw_im = dre_ref[0, 0], dim_ref[0, 0]
    out_re, out_im = [x_re[0:SUBLANES]], [x_im[0:SUBLANES]]
    for blk in range(1, nrows // SUBLANES):
        rows = slice(blk * SUBLANES, (blk + 1) * SUBLANES)
        c_re = jnp.broadcast_to(out_re[-1][SUBLANES - 1:SUBLANES], pw_re.shape)
        c_im = jnp.broadcast_to(out_im[-1][SUBLANES - 1:SUBLANES], pw_re.shape)
        out_re.append(x_re[rows] + pw_re * c_re - pw_im * c_im)
        out_im.append(x_im[rows] + pw_re * c_im + pw_im * c_re)
    return jnp.concatenate(out_re, axis=0), jnp.concatenate(out_im, axis=0)


def _s5_kernel(u_ref, kt_ref, p_ref, q_ref, dre_ref, dim_ref, dskip_ref, o_ref, toep_ref):
    ch, ns = SLICE_CH, SLICE_STATES
    per_blk = MXU_COLS // ch
    nblk = CHUNK // per_blk

    @pl.when(pl.program_id(1) == 0)
    def _():
        for i in range(CHUNK):
            last_j = (i // per_blk + 1) * per_blk
            for j in range(last_j):
                blk = kt_ref[0, 0, i - j] if j <= i else jnp.zeros((ch, ch), BF16)
                toep_ref[j * ch:(j + 1) * ch, i * ch:(i + 1) * ch] = blk

    u_all = jnp.concatenate([u_ref[0, j] for j in range(CHUNK)], axis=-1)
    x_loc = jnp.dot(u_all, p_ref[0, 0], preferred_element_type=F32)
    x_re, x_im = _chunk_scan(x_loc[:, :ns], x_loc[:, ns:], dre_ref, dim_ref)
    x_prev = jnp.concatenate([_shift_rows(x_re, 1), _shift_rows(x_im, 1)], axis=-1).astype(BF16)
    for t in range(nblk):
        cols = slice(t * MXU_COLS, (t + 1) * MXU_COLS)
        k_rows = (t + 1) * MXU_COLS
        y_blk = (jnp.dot(u_all[:, :k_rows], toep_ref[:k_rows, cols], preferred_element_type=F32)
                 + jnp.dot(x_prev, q_ref[0, 0, :, cols], preferred_element_type=F32))
        for s in range(per_blk):
            i = t * per_blk + s
            y = y_blk[:, s * ch:(s + 1) * ch] + u_ref[0, i].astype(F32) * dskip_ref[0]
            o_ref[0, i] = jax.nn.gelu(y).astype(o_ref.dtype)


def _s5(u4, kt, p, q, dre, dim, d_skip, layer):
    bsz, _, nchunk, _ = u4.shape
    ch, ns = SLICE_CH, SLICE_STATES
    nsl = kt.shape[1]
    assert nchunk % SUBLANES == 0
    return pl.pallas_call(
        _s5_kernel,
        out_shape=jax.ShapeDtypeStruct((bsz, CHUNK, nchunk, nsl * ch), BF16),
        grid=(nsl, bsz),
        in_specs=[
            pl.BlockSpec((1, CHUNK, nchunk, ch), lambda s, b: (b, 0, 0, s)),
            pl.BlockSpec((1, 1, CHUNK, ch, ch), lambda s, b: (layer, s, 0, 0, 0)),
            pl.BlockSpec((1, 1, CHUNK * ch, 2 * ns), lambda s, b: (layer, s, 0, 0)),
            pl.BlockSpec((1, 1, 2 * ns, CHUNK * ch), lambda s, b: (layer, s, 0, 0)),
            pl.BlockSpec((1, 1, SUBLANES, ns), lambda s, b: (layer, s, 0, 0)),
            pl.BlockSpec((1, 1, SUBLANES, ns), lambda s, b: (layer, s, 0, 0)),
            pl.BlockSpec((1, 1, ch), lambda s, b: (layer, 0, s)),
        ],
        out_specs=pl.BlockSpec((1, CHUNK, nchunk, ch), lambda s, b: (b, 0, 0, s)),
        scratch_shapes=[pltpu.VMEM((CHUNK * ch, CHUNK * ch), BF16)],
        compiler_params=_params("arbitrary", "arbitrary"),
    )(u4, kt, p, q, dre, dim, d_skip)


def _glu_kernel(a_ref, w_ref, b_ref, y_ref, o_ref, *scratch):
    w = _panel(w_ref, scratch[0] if scratch else None, pl.program_id(1) == 0)

    def block(cols):
        z = jnp.dot(a_ref[...], w[:, cols], preferred_element_type=F32) + b_ref[0, :, cols]
        o_ref[:, cols] = (y_ref[:, cols].astype(F32) * jax.nn.sigmoid(z)).astype(o_ref.dtype)

    _for_col_blocks(o_ref.shape[1], block)


def _glu(y, w, b, layer):
    rows, k = y.shape
    n = w.shape[2]
    tm, tn = min(ROW_TILE, rows), min(2 * COL_TILE, n)
    w_spec, scratch = _weight_specs(w, layer, tn, lambda j: j)
    return pl.pallas_call(
        _glu_kernel,
        out_shape=jax.ShapeDtypeStruct((rows, n), BF16),
        grid=(n // tn, rows // tm),
        in_specs=[
            pl.BlockSpec((tm, k), lambda j, i: (i, 0)),
            w_spec,
            pl.BlockSpec((1, 1, tn), lambda j, i: (layer, 0, j)),
            pl.BlockSpec((tm, tn), lambda j, i: (i, j)),
        ],
        out_specs=pl.BlockSpec((tm, tn), lambda j, i: (i, j)),
        scratch_shapes=scratch,
        compiler_params=_params("arbitrary", "arbitrary"),
    )(y, w, b, y)


def _pool_kernel(v_ref, o_ref, *, blocks_per_group):
    nchunk = v_ref.shape[2]
    group = pl.program_id(1) // blocks_per_group
    first_row = lax.broadcasted_iota(jnp.int32, (nchunk, 1), 0) == 0

    def pooled(win):
        v = [v_ref[0, j].astype(F32) for j in range(CHUNK)]
        prefix = [v[0]]
        for j in range(1, CHUNK):
            prefix.append(prefix[-1] + v[j])
        for j in range(CHUNK):
            if j >= win:
                wsum = prefix[j] - prefix[j - win]
            elif j == win - 1:
                wsum = prefix[j]
            else:
                wsum = prefix[j] + _shift_rows(prefix[CHUNK - 1] - prefix[j - win + CHUNK], 1)
            count = jnp.where(first_row, float(min(j + 1, win)), float(win))
            o_ref[0, j] = (wsum / count - v[j]).astype(o_ref.dtype)

    for g, win in enumerate(POOL_WINDOWS):
        pl.when(group == g)(functools.partial(pooled, win))


def _pool(u4, col0, width):
    bsz, _, nchunk, _ = u4.shape
    tn = 128
    blocks_per_group = width // len(POOL_WINDOWS) // tn
    return pl.pallas_call(
        functools.partial(_pool_kernel, blocks_per_group=blocks_per_group),
        out_shape=jax.ShapeDtypeStruct((bsz, CHUNK, nchunk, width), BF16),
        grid=(bsz, width // tn),
        in_specs=[pl.BlockSpec((1, CHUNK, nchunk, tn), lambda b, t: (b, 0, 0, col0 // tn + t))],
        out_specs=pl.BlockSpec((1, CHUNK, nchunk, tn), lambda b, t: (b, 0, 0, t)),
        compiler_params=_params("parallel", "parallel"),
    )(u4)


def _pool_mix_kernel(a_ref, w_ref, b_ref, s_ref, o_ref):
    z = jnp.dot(a_ref[...], w_ref[0, 0].astype(BF16), preferred_element_type=F32)
    o_ref[...] = ((z + b_ref[0, 0]) * s_ref[0]).astype(o_ref.dtype)


def _pool_mix(pooled, w, b, scale, layer):
    rows, width = pooled.shape
    ngroups, gw = w.shape[1], w.shape[2]
    tm = min(2 * ROW_TILE, rows)
    return pl.pallas_call(
        _pool_mix_kernel,
        out_shape=jax.ShapeDtypeStruct((rows, width), BF16),
        grid=(ngroups, rows // tm),
        in_specs=[
            pl.BlockSpec((tm, gw), lambda g, i: (i, g)),
            pl.BlockSpec((1, 1, gw, gw), lambda g, i: (layer, g, 0, 0)),
            pl.BlockSpec((1, 1, 1, gw), lambda g, i: (layer, g, 0, 0)),
            pl.BlockSpec((1, 1, gw), lambda g, i: (layer, 0, g)),
        ],
        out_specs=pl.BlockSpec((tm, gw), lambda g, i: (i, g)),
        compiler_params=_params("parallel", "parallel"),
    )(pooled, w, b, scale)


def _merge_kernel(a1_ref, w1_ref, a2_ref, w2_ref, g1_ref, g2_ref, o_ref, *scratch):
    first = pl.program_id(1) == 0
    w1 = _panel(w1_ref, scratch[0] if scratch else None, first)
    w2 = _panel(w2_ref, scratch[1] if scratch else None, first)

    def block(cols):
        z1 = jnp.dot(a1_ref[...], w1[:, cols], preferred_element_type=F32)
        z2 = jnp.dot(a2_ref[...], w2[:, cols], preferred_element_type=F32)
        o_ref[:, cols] = (g1_ref[:, cols].astype(F32) * z1 + g2_ref[:, cols].astype(F32) * z2).astype(o_ref.dtype)

    _for_col_blocks(o_ref.shape[1], block)


def _merge(o_ssm, w_ssm_out, o_pool, w_pool_out, gates, layer):
    rows, k1 = o_ssm.shape
    k2 = o_pool.shape[1]
    n = w_ssm_out.shape[2]
    tm, tn = min(ROW_TILE, rows), min(COL_TILE, n)
    w1_spec, scratch1 = _weight_specs(w_ssm_out, layer, tn, lambda j: j)
    w2_spec, scratch2 = _weight_specs(w_pool_out, layer, tn, lambda j: j)
    return pl.pallas_call(
        _merge_kernel,
        out_shape=jax.ShapeDtypeStruct((rows, n), BF16),
        grid=(n // tn, rows // tm),
        in_specs=[
            pl.BlockSpec((tm, k1), lambda j, i: (i, 0)),
            w1_spec,
            pl.BlockSpec((tm, k2), lambda j, i: (i, 0)),
            w2_spec,
            pl.BlockSpec((tm, tn), lambda j, i: (i, j)),
            pl.BlockSpec((tm, tn), lambda j, i: (i, n // tn + j)),
        ],
        out_specs=pl.BlockSpec((tm, tn), lambda j, i: (i, j)),
        scratch_shapes=scratch1 + scratch2,
        compiler_params=_params("arbitrary", "arbitrary"),
    )(o_ssm, w_ssm_out, o_pool, w_pool_out, gates, gates)


def _residual_kernel(a_ref, w_ref, h_ref, g_ref, o_ref, *scratch):
    w = _panel(w_ref, scratch[0] if scratch else None, pl.program_id(1) == 0)

    def block(cols):
        z = jnp.dot(a_ref[...], w[:, cols], preferred_element_type=F32)
        o_ref[:, cols] = h_ref[:, cols] + g_ref[0, :, cols] * z

    _for_col_blocks(o_ref.shape[1], block)


def _residual(a, w, layer, h, gate, *, tm, tn):
    rows, k = a.shape
    n = w.shape[2]
    rows_per_seq = rows // gate.shape[0]
    tm, tn = min(tm, rows_per_seq), min(tn, n)
    w_spec, scratch = _weight_specs(w, layer, tn, lambda j: j)
    return pl.pallas_call(
        _residual_kernel,
        out_shape=jax.ShapeDtypeStruct((rows, n), F32),
        grid=(n // tn, rows // tm),
        in_specs=[
            pl.BlockSpec((tm, k), lambda j, i: (i, 0)),
            w_spec,
            pl.BlockSpec((tm, tn), lambda j, i: (i, j)),
            pl.BlockSpec((1, 1, tn), lambda j, i: (i * tm // rows_per_seq, 0, j)),
        ],
        out_specs=pl.BlockSpec((tm, tn), lambda j, i: (i, j)),
        scratch_shapes=scratch,
        input_output_aliases={2: 0},
        compiler_params=_params("arbitrary", "arbitrary"),
    )(a, w, h, gate)


def _up_conv_kernel(a_ref, wg_ref, wv_ref, cwg_ref, cwv_ref, cbg_ref, cbv_ref, o_ref, carry_ref, *scratch):
    _, _, cm, d = a_ref.shape
    tn = o_ref.shape[-1]
    seq_start = pl.program_id(2) == 0
    first = jnp.logical_and(pl.program_id(1) == 0, seq_start)
    wg = _panel(wg_ref, scratch[0] if scratch else None, first)
    wv = _panel(wv_ref, scratch[1] if scratch else None, first)

    @pl.when(seq_start)
    def _():
        carry_ref[...] = jnp.zeros(carry_ref.shape, carry_ref.dtype)

    a = a_ref[0].reshape(CHUNK * cm, d)

    products = {}

    def matmuls(cols):
        for name, w in (("gate", wg), ("val", wv)):
            products[name, cols.start] = jnp.dot(a, w[:, cols], preferred_element_type=F32)

    _for_col_blocks(tn, matmuls)

    def conv(name, cw_ref, cb_ref, slot, cols):
        sub = cols.stop - cols.start
        first_row = lax.broadcasted_iota(jnp.int32, (cm, sub), 0) == 0
        r = products[name, cols.start].reshape(CHUNK, cm, sub)
        prev = []
        for t in range(CONV_WIDTH - 1):
            plane = r[CHUNK - (CONV_WIDTH - 1) + t]
            prev.append(jnp.where(first_row, carry_ref[slot + t, 0:1, cols], pltpu.roll(plane, 1, 0)))
            carry_ref[slot + t, :, cols] = jnp.broadcast_to(plane[cm - 1:cm, :], (SUBLANES, sub))
        planes = prev + [r[j] for j in range(CHUNK)]
        outs = []
        for j in range(CHUNK):
            acc = cb_ref[0, :, cols]
            for t in range(CONV_WIDTH):
                acc = acc + cw_ref[0, t:t + 1, cols] * planes[j + t]
            outs.append(acc)
        return outs

    def block(cols):
        gate = conv("gate", cwg_ref, cbg_ref, 0, cols)
        val = conv("val", cwv_ref, cbv_ref, CONV_WIDTH - 1, cols)
        for j in range(CHUNK):
            o_ref[0, j, :, cols] = (jax.nn.silu(gate[j]) * val[j]).astype(o_ref.dtype)

    _for_col_blocks(tn, block)


def _up_conv(y4, w_up, conv_w, conv_b, layer):
    bsz, _, nchunk, d = y4.shape
    d_ff = w_up.shape[2] // 2
    cm, tn = min(ROW_TILE // 2 // CHUNK, nchunk), min(COL_TILE, d_ff)
    nblk = d_ff // tn
    scratch = [] if w_up.dtype == BF16 else [pltpu.VMEM((d, tn), BF16)] * 2
    return pl.pallas_call(
        _up_conv_kernel,
        out_shape=jax.ShapeDtypeStruct((bsz, CHUNK, nchunk, d_ff), BF16),
        grid=(nblk, bsz, nchunk // cm),
        in_specs=[
            pl.BlockSpec((1, CHUNK, cm, d), lambda n, b, c: (b, 0, c, 0)),
            pl.BlockSpec((1, d, tn), lambda n, b, c: (layer, 0, n)),
            pl.BlockSpec((1, d, tn), lambda n, b, c: (layer, 0, nblk + n)),
            pl.BlockSpec((1, CONV_WIDTH, tn), lambda n, b, c: (layer, 0, n)),
            pl.BlockSpec((1, CONV_WIDTH, tn), lambda n, b, c: (layer, 0, nblk + n)),
            pl.BlockSpec((1, 1, tn), lambda n, b, c: (layer, 0, n)),
            pl.BlockSpec((1, 1, tn), lambda n, b, c: (layer, 0, nblk + n)),
        ],
        out_specs=pl.BlockSpec((1, CHUNK, cm, tn), lambda n, b, c: (b, 0, c, n)),
        scratch_shapes=[pltpu.VMEM((2 * (CONV_WIDTH - 1), SUBLANES, tn), F32)] + scratch,
        compiler_params=_params("arbitrary", "arbitrary", "arbitrary"),
    )(y4, w_up, w_up, conv_w, conv_w, conv_b, conv_b)


def kernel(x, c, w_cond, b_cond, ada_table, norm1_g, norm2_g, w_in, ssm_a_re, ssm_a_im, ssm_log_dt, ssm_b_re, ssm_b_im, ssm_c_re, ssm_c_im, ssm_d, w_glu, b_glu, w_pool, b_pool, pool_scale, w_ssm_out, w_pool_out, w_o, w_up, conv_w, conv_b, w_down, final_g):
    bsz, seq, d = x.shape
    depth = w_in.shape[0]
    nchunk = seq // CHUNK
    rows = bsz * seq
    ssm_width = w_glu.shape[2]
    pool_width = pool_scale.shape[1]
    assert seq % CHUNK == 0 and ssm_width % SLICE_CH == 0

    c_pad = jnp.pad(c, ((0, SUBLANES - bsz), (0, 0)))
    mods = _cond(c_pad, w_cond, b_cond, ada_table)[:, :bsz].reshape(depth, bsz, N_MOD, 1, d)
    kt, p, q, dre, dim = _s5_prep(ssm_a_re, ssm_a_im, ssm_log_dt, ssm_b_re, ssm_b_im, ssm_c_re, ssm_c_im)
    d_skip = ssm_d.reshape(depth, 1, ssm_width)
    b_glu = b_glu.reshape(depth, 1, ssm_width)
    b_pool = b_pool[:, :, None, :]
    pool_scale = pool_scale.reshape(depth, 1, pool_width)
    conv_b = conv_b.reshape(depth, 1, -1)
    w_down = w_down.astype(BF16)

    h = None
    for l in range(depth):
        shift1, scale1, gate1, shift2, scale2, gate2 = (mods[l, :, i] for i in range(N_MOD))
        if l == 0:
            y, h = _norm(x, norm1_g[l], scale1, shift1, in_tokens=True, copy_out=True)
            h = h.reshape(rows, d)
        else:
            y = _norm(h.reshape(bsz, CHUNK, nchunk, d), norm1_g[l], scale1, shift1)
        y = y.reshape(rows, d)
        u = _proj(y, w_in, l, 0, ssm_width + pool_width, sigmoid=False)
        gates = _proj(y, w_in, l, ssm_width + pool_width, 2 * d, sigmoid=True)
        u4 = u.reshape(bsz, CHUNK, nchunk, -1)
        y_ssm = _s5(u4, kt, p, q, dre, dim, d_skip, l).reshape(rows, ssm_width)
        o_ssm = _glu(y_ssm, w_glu, b_glu, l)
        pooled = _pool(u4, ssm_width, pool_width).reshape(rows, pool_width)
        o_pool = _pool_mix(pooled, w_pool, b_pool, pool_scale, l)
        merged = _merge(o_ssm, w_ssm_out, o_pool, w_pool_out, gates, l)
        h = _residual(merged, w_o, l, h, gate1, tm=ROW_TILE, tn=COL_TILE)

        y = _norm(h.reshape(bsz, CHUNK, nchunk, d), norm2_g[l], scale2, shift2)
        act = _up_conv(y, w_up, conv_w, conv_b, l).reshape(rows, -1)
        h = _residual(act, w_down, l, h, gate2, tm=ROW_TILE // 2, tn=COL_TILE)

    return _norm(h.reshape(bsz, CHUNK, nchunk, d), final_g, None, None, out_tokens=True, out_dtype=x.dtype)
```

```python
import functools

import jax
import jax.numpy as jnp
from jax import lax
from jax.experimental import pallas as pl
from jax.experimental.pallas import tpu as pltpu

F32 = jnp.float32
BF16 = jnp.bfloat16

CHUNK = 16
SSM_GROUP = 16
SSM_STATE = 64
SLICE_CH = 128
SLICE_GROUPS = SLICE_CH // SSM_GROUP
SLICE_STATES = SLICE_GROUPS * SSM_STATE
POOL_WINDOWS = (2, 4, 8, 16)
CONV_WIDTH = 3
N_MOD = 6
RMS_EPS = 1e-6
LAM_RE_MAX = -1e-4
SUBLANES = 8
LANES = 128
MXU_COLS = 256
V7X_VMEM_LIMIT = 56 * 1024 * 1024
ROW_TILE = 1024
COL_TILE = 1024
MAX_WEIGHT_CHUNKS = 16


def _params(*sem):
    return pltpu.CompilerParams(dimension_semantics=sem, vmem_limit_bytes=V7X_VMEM_LIMIT)


def _shift_rows(x, sh):
    rolled = pltpu.roll(x, sh, 0)
    idx = lax.broadcasted_iota(jnp.int32, x.shape, 0)
    return jnp.where(idx < sh, jnp.zeros_like(x), rolled)


class _WeightStream:
    def __init__(self, w_hbm, wbf_ref, stage_ref, sem, *, layer, col_block0, panel, n_panels, step):
        self.w_hbm, self.wbf, self.stage, self.sem = w_hbm, wbf_ref, stage_ref, sem
        self.layer, self.col_block0 = layer, col_block0
        self.panel, self.step = panel, step
        _, self.k, self.tn = wbf_ref.shape
        self.kc = stage_ref.shape[1]
        self.n_chunks = self.k // self.kc
        self.streaming = jnp.logical_and(panel + 1 < n_panels, step < self.n_chunks)

    def _rows(self, chunk):
        start = chunk * self.kc
        return pl.ds(start if isinstance(start, int) else pl.multiple_of(start, self.kc), self.kc)

    def _copy(self, panel, chunk, slot):
        col = (self.col_block0 + panel) * self.tn
        cols = pl.ds(col if isinstance(col, int) else pl.multiple_of(col, self.tn), self.tn)
        return pltpu.make_async_copy(self.w_hbm.at[self.layer, self._rows(chunk), cols],
                                     self.stage.at[slot], self.sem.at[slot])

    def _cast(self, wbf_slot, chunk, slot):
        self.wbf[wbf_slot, self._rows(chunk), :] = self.stage[slot].astype(BF16)

    def begin(self):
        @pl.when(jnp.logical_and(self.panel == 0, self.step == 0))
        def _():
            self._copy(0, 0, 0).start()
            for c in range(self.n_chunks):
                if c + 1 < self.n_chunks:
                    self._copy(0, c + 1, (c + 1) % 2).start()
                self._copy(0, c, c % 2).wait()
                self._cast(0, c, c % 2)

        @pl.when(self.streaming)
        def _():
            self._copy(self.panel + 1, self.step, 0).start()

        return self.wbf.at[self.panel % 2]

    def finish(self):
        @pl.when(self.streaming)
        def _():
            self._copy(self.panel + 1, self.step, 0).wait()
            self._cast((self.panel + 1) % 2, self.step, 0)


def _stream_scratch(k, tn, steps):
    n_chunks = min(steps, MAX_WEIGHT_CHUNKS)
    return [pltpu.VMEM((2, k, tn), BF16), pltpu.VMEM((2, k // n_chunks, tn), F32), pltpu.SemaphoreType.DMA((2,))]


HBM_SPEC = pl.BlockSpec(memory_space=pl.ANY)


def _for_col_blocks(width, body):
    sub = min(MXU_COLS, width)
    for j in range(width // sub):
        body(slice(j * sub, (j + 1) * sub))


def _cond_kernel(c_ref, w_ref, b_ref, ada_ref, o_ref):
    a = jax.nn.silu(c_ref[...]).astype(BF16)
    z = jnp.dot(a, w_ref[...].astype(BF16), preferred_element_type=F32) + b_ref[...]
    o_ref[...] = z[None] + ada_ref[...]


def _cond(c_pad, w_cond, b_cond, ada_table):
    rows, d = c_pad.shape
    n = w_cond.shape[1]
    depth = ada_table.shape[0]
    tn = min(COL_TILE // 2, n)
    return pl.pallas_call(
        _cond_kernel,
        out_shape=jax.ShapeDtypeStruct((depth, rows, n), F32),
        grid=(n // tn,),
        in_specs=[
            pl.BlockSpec((rows, d), lambda i: (0, 0)),
            pl.BlockSpec((d, tn), lambda i: (0, i)),
            pl.BlockSpec((1, tn), lambda i: (0, i)),
            pl.BlockSpec((depth, 1, tn), lambda i: (0, 0, i)),
        ],
        out_specs=pl.BlockSpec((depth, rows, tn), lambda i: (0, 0, i)),
        compiler_params=_params("arbitrary"),
    )(c_pad, w_cond, b_cond.reshape(1, n), ada_table.reshape(depth, 1, n))


def _norm_kernel(*refs, modulated, in_tokens, out_tokens, copy_out):
    refs = list(refs)
    x_ref, g_ref = refs[:2]
    scale_ref, shift_ref = refs[2:4] if modulated else (None, None)
    outs = refs[4 if modulated else 2:]
    o_ref = outs[0]
    copy_ref = outs[1] if copy_out else None
    stage_ref = outs[-1] if (in_tokens or out_tokens) else None
    tc = (x_ref.shape[1] // CHUNK) if in_tokens else x_ref.shape[2]
    lane_tiles = x_ref.shape[-1] // LANES
    plane_rows = lambda j: pl.ds(j, tc, stride=CHUNK)
    if in_tokens:
        for k in range(lane_tiles):
            stage_ref[k] = x_ref[0, :, k * LANES:(k + 1) * LANES]
    for j in range(CHUNK):
        if in_tokens:
            x = jnp.concatenate([stage_ref[k, plane_rows(j), :] for k in range(lane_tiles)], axis=-1)
        else:
            x = x_ref[0, j]
        y = x * lax.rsqrt(jnp.mean(x * x, axis=-1, keepdims=True) + RMS_EPS) * g_ref[...]
        if modulated:
            y = y * (1.0 + scale_ref[0]) + shift_ref[0]
        if out_tokens:
            for k in range(lane_tiles):
                stage_ref[k, plane_rows(j), :] = y[:, k * LANES:(k + 1) * LANES]
        else:
            o_ref[0, j] = y.astype(o_ref.dtype)
        if copy_out:
            copy_ref[0, j] = x
    if out_tokens:
        for k in range(lane_tiles):
            o_ref[0, :, k * LANES:(k + 1) * LANES] = stage_ref[k].astype(o_ref.dtype)


def _norm(x, g, scale, shift, *, in_tokens=False, out_tokens=False, copy_out=False, out_dtype=BF16):
    if in_tokens:
        bsz, seq, d = x.shape
        nchunk = seq // CHUNK
    else:
        bsz, _, nchunk, d = x.shape
    tc = min(16, nchunk)
    tok_spec = pl.BlockSpec((1, tc * CHUNK, d), lambda b, c: (b, c, 0))
    pos_spec = pl.BlockSpec((1, CHUNK, tc, d), lambda b, c: (b, 0, c, 0))
    modulated = scale is not None
    in_specs = [tok_spec if in_tokens else pos_spec, pl.BlockSpec((1, d), lambda b, c: (0, 0))]
    args = [x, g.reshape(1, d)]
    if modulated:
        mod_spec = pl.BlockSpec((1, 1, d), lambda b, c: (b, 0, 0))
        in_specs += [mod_spec, mod_spec]
        args += [scale, shift]
    pos_shape = (bsz, CHUNK, nchunk, d)
    out_shape = [jax.ShapeDtypeStruct((bsz, nchunk * CHUNK, d) if out_tokens else pos_shape, out_dtype)]
    out_specs = [tok_spec if out_tokens else pos_spec]
    if copy_out:
        out_shape.append(jax.ShapeDtypeStruct(pos_shape, F32))
        out_specs.append(pos_spec)
    staged = in_tokens or out_tokens
    scratch = [pltpu.VMEM((d // LANES, tc * CHUNK, LANES), F32)] if staged else []
    outs = pl.pallas_call(
        functools.partial(_norm_kernel, modulated=modulated, in_tokens=in_tokens, out_tokens=out_tokens,
                          copy_out=copy_out),
        out_shape=out_shape,
        grid=(bsz, nchunk // tc),
        in_specs=in_specs,
        out_specs=out_specs,
        scratch_shapes=scratch,
        compiler_params=_params("parallel", "parallel"),
    )(*args)
    return outs if copy_out else outs[0]


def _grid_stream(w_hbm, scratch, layer, col_block0=0):
    return _WeightStream(w_hbm, *scratch, layer=layer, col_block0=col_block0, panel=pl.program_id(0),
                         n_panels=pl.num_programs(0), step=pl.program_id(1))


def _proj_kernel(a_ref, w_hbm, o_ref, *scratch, sigmoid, layer, col_block0):
    stream = _grid_stream(w_hbm, scratch, layer, col_block0)
    w = stream.begin()

    def block(cols):
        z = jnp.dot(a_ref[...], w[:, cols], preferred_element_type=F32)
        o_ref[:, cols] = (jax.nn.sigmoid(z) if sigmoid else z).astype(o_ref.dtype)

    _for_col_blocks(o_ref.shape[1], block)
    stream.finish()


def _proj(a, w, layer, col0, n, *, sigmoid):
    rows, k = a.shape
    tm, tn = min(ROW_TILE, rows), min(COL_TILE, n)
    return pl.pallas_call(
        functools.partial(_proj_kernel, sigmoid=sigmoid, layer=layer, col_block0=col0 // tn),
        out_shape=jax.ShapeDtypeStruct((rows, n), BF16),
        grid=(n // tn, rows // tm),
        in_specs=[pl.BlockSpec((tm, k), lambda j, i: (i, 0)), HBM_SPEC],
        out_specs=pl.BlockSpec((tm, tn), lambda j, i: (i, j)),
        scratch_shapes=_stream_scratch(k, tn, rows // tm),
        compiler_params=_params("arbitrary", "arbitrary"),
    )(a, w)


def _discretise(a_re, a_im, log_dt):
    lam_re = jnp.minimum(a_re, LAM_RE_MAX)
    lam_im = a_im
    dt = jnp.exp(log_dt)
    mag = jnp.exp(lam_re * dt)
    abar_re = mag * jnp.cos(lam_im * dt)
    abar_im = mag * jnp.sin(lam_im * dt)
    den = lam_re * lam_re + lam_im * lam_im
    x_re = abar_re - 1.0
    f_re = (x_re * lam_re + abar_im * lam_im) / den
    f_im = (abar_im * lam_re - x_re * lam_im) / den
    return abar_re, abar_im, f_re, f_im


def _split_bf16(x):
    hi = x.astype(BF16)
    return hi, (x - hi.astype(F32)).astype(BF16)


def _dot_split(a, b):
    a_hi, a_lo = a
    b_hi, b_lo = b
    dot = functools.partial(jnp.dot, preferred_element_type=F32)
    return dot(a_hi, b_hi) + (dot(a_hi, b_lo) + dot(a_lo, b_hi))


def _s5_prep_kernel(rowp_ref, colp_ref, bt_ref, ct_ref, kt_ref, p_ref, q_ref, dre_ref, dim_ref):
    ns, ch = SLICE_STATES, SLICE_CH
    ar, ai, f_re, f_im = _discretise(rowp_ref[0, 0, 0:1, :], rowp_ref[0, 0, 1:2, :], rowp_ref[0, 0, 2:3, :])
    bt_re = jnp.concatenate([bt_ref[0, 0, 0]] * SLICE_GROUPS, axis=0)
    bt_im = jnp.concatenate([bt_ref[0, 0, 1]] * SLICE_GROUPS, axis=0)
    same_group = (lax.broadcasted_iota(jnp.int32, (ch, ns), 0) // SSM_GROUP
                  == lax.broadcasted_iota(jnp.int32, (ch, ns), 1) // SSM_STATE)
    e_re = jnp.where(same_group, f_re * bt_re - f_im * bt_im, 0.0)
    e_im = jnp.where(same_group, f_re * bt_im + f_im * bt_re, 0.0)
    same_group_t = (lax.broadcasted_iota(jnp.int32, (ns, ch), 0) // SSM_STATE
                    == lax.broadcasted_iota(jnp.int32, (ns, ch), 1) // SSM_GROUP)
    c_re = jnp.where(same_group_t, ct_ref[0, 0, 0], 0.0)
    c_im = jnp.where(same_group_t, ct_ref[0, 0, 1], 0.0)
    c_re_split, c_im_split = _split_bf16(c_re), _split_bf16(c_im)
    for t in range(CHUNK):
        j = CHUNK - 1 - t
        e_re_split, e_im_split = _split_bf16(e_re), _split_bf16(e_im)
        p_ref[0, 0, j * ch:(j + 1) * ch, 0:ns] = e_re_split[0]
        p_ref[0, 0, j * ch:(j + 1) * ch, ns:2 * ns] = e_im_split[0]
        kt_ref[0, 0, t] = (_dot_split(e_re_split, c_re_split) - _dot_split(e_im_split, c_im_split)).astype(BF16)
        e_re, e_im = e_re * ar - e_im * ai, e_re * ai + e_im * ar

    d_re, d_im = ar, ai
    for _ in range(CHUNK.bit_length() - 1):
        d_re, d_im = d_re * d_re - d_im * d_im, 2.0 * d_re * d_im
    p_re, p_im = d_re, d_im
    for r in range(SUBLANES):
        dre_ref[0, 0, r:r + 1, :] = p_re
        dim_ref[0, 0, r:r + 1, :] = p_im
        p_re, p_im = p_re * d_re - p_im * d_im, p_re * d_im + p_im * d_re

    arc, aic, _, _ = _discretise(colp_ref[0, 0, 0], colp_ref[0, 0, 1], colp_ref[0, 0, 2])
    w_re, w_im = c_re, c_im
    for i in range(CHUNK):
        w_re, w_im = w_re * arc - w_im * aic, w_re * aic + w_im * arc
        q_ref[0, 0, 0:ns, i * ch:(i + 1) * ch] = w_re.astype(BF16)
        q_ref[0, 0, ns:2 * ns, i * ch:(i + 1) * ch] = (-w_im).astype(BF16)


def _s5_prep(a_re, a_im, log_dt, b_re, b_im, c_re, c_im):
    depth, groups, nst = a_re.shape
    nsl = groups // SLICE_GROUPS
    ns, ch = SLICE_STATES, SLICE_CH
    dt_b = jnp.broadcast_to(log_dt[:, :, None], a_re.shape)
    rows = jnp.stack([a_re, a_im, dt_b], axis=2).reshape(depth, nsl, SLICE_GROUPS, 3, nst)
    rows = jnp.swapaxes(rows, 2, 3).reshape(depth, nsl, 3, ns)
    rowp = jnp.pad(rows, ((0, 0), (0, 0), (0, SUBLANES - 3), (0, 0)))
    colp = jnp.broadcast_to(rows[..., None], (depth, nsl, 3, ns, ch))
    bt = jnp.stack([b_re, b_im], axis=1).reshape(depth, 2, nsl, ns, SSM_GROUP)
    bt = jnp.transpose(bt, (0, 2, 1, 4, 3))
    ct = jnp.stack([c_re, c_im], axis=1).reshape(depth, 2, nsl, SLICE_GROUPS, SSM_GROUP, nst)
    ct = jnp.transpose(ct, (0, 2, 1, 3, 5, 4)).reshape(depth, nsl, 2, ns, SSM_GROUP)
    ct = jnp.tile(ct, (1, 1, 1, 1, SLICE_GROUPS))
    out_shapes = (
        jax.ShapeDtypeStruct((depth, nsl, CHUNK, ch, ch), BF16),
        jax.ShapeDtypeStruct((depth, nsl, CHUNK * ch, 2 * ns), BF16),
        jax.ShapeDtypeStruct((depth, nsl, 2 * ns, CHUNK * ch), BF16),
        jax.ShapeDtypeStruct((depth, nsl, SUBLANES, ns), F32),
        jax.ShapeDtypeStruct((depth, nsl, SUBLANES, ns), F32),
    )
    return pl.pallas_call(
        _s5_prep_kernel,
        out_shape=out_shapes,
        grid=(depth, nsl),
        in_specs=[
            pl.BlockSpec((1, 1, SUBLANES, ns), lambda l, s: (l, s, 0, 0)),
            pl.BlockSpec((1, 1, 3, ns, ch), lambda l, s: (l, s, 0, 0, 0)),
            pl.BlockSpec((1, 1, 2, SSM_GROUP, ns), lambda l, s: (l, s, 0, 0, 0)),
            pl.BlockSpec((1, 1, 2, ns, ch), lambda l, s: (l, s, 0, 0, 0)),
        ],
        out_specs=(
            pl.BlockSpec((1, 1, CHUNK, ch, ch), lambda l, s: (l, s, 0, 0, 0)),
            pl.BlockSpec((1, 1, CHUNK * ch, 2 * ns), lambda l, s: (l, s, 0, 0)),
            pl.BlockSpec((1, 1, 2 * ns, CHUNK * ch), lambda l, s: (l, s, 0, 0)),
            pl.BlockSpec((1, 1, SUBLANES, ns), lambda l, s: (l, s, 0, 0)),
            pl.BlockSpec((1, 1, SUBLANES, ns), lambda l, s: (l, s, 0, 0)),
        ),
        compiler_params=_params("parallel", "parallel"),
    )(rowp, colp, bt, ct)


def _chunk_scan(x_re, x_im, dre_ref, dim_ref):
    nrows = x_re.shape[0]
    row_in_block = lax.broadcasted_iota(jnp.int32, x_re.shape, 0) % SUBLANES
    sh = 1
    while sh < SUBLANES:
        d_re, d_im = dre_ref[0, 0, sh - 1:sh, :], dim_ref[0, 0, sh - 1:sh, :]
        keep = row_in_block >= sh
        s_re = jnp.where(keep, pltpu.roll(x_re, sh, 0), 0.0)
        s_im = jnp.where(keep, pltpu.roll(x_im, sh, 0), 0.0)
        x_re, x_im = x_re + d_re * s_re - d_im * s_im, x_im + d_re * s_im + d_im * s_re
        sh *= 2
    pw_re, pw_im = dre_ref[0, 0], dim_ref[0, 0]
    out_re, out_im = [x_re[0:SUBLANES]], [x_im[0:SUBLANES]]
    for blk in range(1, nrows // SUBLANES):
        rows = slice(blk * SUBLANES, (blk + 1) * SUBLANES)
        c_re = jnp.broadcast_to(out_re[-1][SUBLANES - 1:SUBLANES], pw_re.shape)
        c_im = jnp.broadcast_to(out_im[-1][SUBLANES - 1:SUBLANES], pw_re.shape)
        out_re.append(x_re[rows] + pw_re * c_re - pw_im * c_im)
        out_im.append(x_im[rows] + pw_re * c_im + pw_im * c_re)
    return jnp.concatenate(out_re, axis=0), jnp.concatenate(out_im, axis=0)


def _s5_kernel(u_ref, kt_ref, p_ref, q_ref, dre_ref, dim_ref, dskip_ref, o_ref, toep_ref):
    ch, ns = SLICE_CH, SLICE_STATES
    per_blk = MXU_COLS // ch
    nblk = CHUNK // per_blk

    @pl.when(pl.program_id(1) == 0)
    def _():
        for i in range(CHUNK):
            last_j = (i // per_blk + 1) * per_blk
            for j in range(last_j):
                blk = kt_ref[0, 0, i - j] if j <= i else jnp.zeros((ch, ch), BF16)
                toep_ref[j * ch:(j + 1) * ch, i * ch:(i + 1) * ch] = blk

    u_all = jnp.concatenate([u_ref[0, j] for j in range(CHUNK)], axis=-1)
    x_loc = jnp.dot(u_all, p_ref[0, 0], preferred_element_type=F32)
    x_re, x_im = _chunk_scan(x_loc[:, :ns], x_loc[:, ns:], dre_ref, dim_ref)
    x_prev = jnp.concatenate([_shift_rows(x_re, 1), _shift_rows(x_im, 1)], axis=-1).astype(BF16)
    for t in range(nblk):
        cols = slice(t * MXU_COLS, (t + 1) * MXU_COLS)
        k_rows = (t + 1) * MXU_COLS
        y_blk = (jnp.dot(u_all[:, :k_rows], toep_ref[:k_rows, cols], preferred_element_type=F32)
                 + jnp.dot(x_prev, q_ref[0, 0, :, cols], preferred_element_type=F32))
        for s in range(per_blk):
            i = t * per_blk + s
            y = y_blk[:, s * ch:(s + 1) * ch] + u_ref[0, i].astype(F32) * dskip_ref[0]
            o_ref[0, i] = jax.nn.gelu(y).astype(o_ref.dtype)


def _s5(u4, kt, p, q, dre, dim, d_skip, layer):
    bsz, _, nchunk, _ = u4.shape
    ch, ns = SLICE_CH, SLICE_STATES
    nsl = kt.shape[1]
    assert nchunk % SUBLANES == 0
    return pl.pallas_call(
        _s5_kernel,
        out_shape=jax.ShapeDtypeStruct((bsz, CHUNK, nchunk, nsl * ch), BF16),
        grid=(nsl, bsz),
        in_specs=[
            pl.BlockSpec((1, CHUNK, nchunk, ch), lambda s, b: (b, 0, 0, s)),
            pl.BlockSpec((1, 1, CHUNK, ch, ch), lambda s, b: (layer, s, 0, 0, 0)),
            pl.BlockSpec((1, 1, CHUNK * ch, 2 * ns), lambda s, b: (layer, s, 0, 0)),
            pl.BlockSpec((1, 1, 2 * ns, CHUNK * ch), lambda s, b: (layer, s, 0, 0)),
            pl.BlockSpec((1, 1, SUBLANES, ns), lambda s, b: (layer, s, 0, 0)),
            pl.BlockSpec((1, 1, SUBLANES, ns), lambda s, b: (layer, s, 0, 0)),
            pl.BlockSpec((1, 1, ch), lambda s, b: (layer, 0, s)),
        ],
        out_specs=pl.BlockSpec((1, CHUNK, nchunk, ch), lambda s, b: (b, 0, 0, s)),
        scratch_shapes=[pltpu.VMEM((CHUNK * ch, CHUNK * ch), BF16)],
        compiler_params=_params("arbitrary", "arbitrary"),
    )(u4, kt, p, q, dre, dim, d_skip)


def _glu_kernel(a_ref, w_hbm, b_ref, y_ref, o_ref, *scratch, layer):
    stream = _grid_stream(w_hbm, scratch, layer)
    w = stream.begin()

    def block(cols):
        z = jnp.dot(a_ref[...], w[:, cols], preferred_element_type=F32) + b_ref[0, :, cols]
        o_ref[:, cols] = (y_ref[:, cols].astype(F32) * jax.nn.sigmoid(z)).astype(o_ref.dtype)

    _for_col_blocks(o_ref.shape[1], block)
    stream.finish()


def _glu(y, w, b, layer):
    rows, k = y.shape
    n = w.shape[2]
    tm, tn = min(ROW_TILE, rows), min(COL_TILE, n)
    return pl.pallas_call(
        functools.partial(_glu_kernel, layer=layer),
        out_shape=jax.ShapeDtypeStruct((rows, n), BF16),
        grid=(n // tn, rows // tm),
        in_specs=[
            pl.BlockSpec((tm, k), lambda j, i: (i, 0)),
            HBM_SPEC,
            pl.BlockSpec((1, 1, tn), lambda j, i: (layer, 0, j)),
            pl.BlockSpec((tm, tn), lambda j, i: (i, j)),
        ],
        out_specs=pl.BlockSpec((tm, tn), lambda j, i: (i, j)),
        scratch_shapes=_stream_scratch(k, tn, rows // tm),
        compiler_params=_params("arbitrary", "arbitrary"),
    )(y, w, b, y)


def _pool_kernel(v_ref, o_ref, *, blocks_per_group):
    nchunk = v_ref.shape[2]
    group = pl.program_id(1) // blocks_per_group
    first_row = lax.broadcasted_iota(jnp.int32, (nchunk, 1), 0) == 0

    def pooled(win):
        v = [v_ref[0, j].astype(F32) for j in range(CHUNK)]
        prefix = [v[0]]
        for j in range(1, CHUNK):
            prefix.append(prefix[-1] + v[j])
        for j in range(CHUNK):
            if j >= win:
                wsum = prefix[j] - prefix[j - win]
            elif j == win - 1:
                wsum = prefix[j]
            else:
                wsum = prefix[j] + _shift_rows(prefix[CHUNK - 1] - prefix[j - win + CHUNK], 1)
            count = jnp.where(first_row, float(min(j + 1, win)), float(win))
            o_ref[0, j] = (wsum / count - v[j]).astype(o_ref.dtype)

    for g, win in enumerate(POOL_WINDOWS):
        pl.when(group == g)(functools.partial(pooled, win))


def _pool(u4, col0, width):
    bsz, _, nchunk, _ = u4.shape
    tn = 128
    blocks_per_group = width // len(POOL_WINDOWS) // tn
    return pl.pallas_call(
        functools.partial(_pool_kernel, blocks_per_group=blocks_per_group),
        out_shape=jax.ShapeDtypeStruct((bsz, CHUNK, nchunk, width), BF16),
        grid=(bsz, width // tn),
        in_specs=[pl.BlockSpec((1, CHUNK, nchunk, tn), lambda b, t: (b, 0, 0, col0 // tn + t))],
        out_specs=pl.BlockSpec((1, CHUNK, nchunk, tn), lambda b, t: (b, 0, 0, t)),
        compiler_params=_params("parallel", "parallel"),
    )(u4)


def _pool_mix_kernel(a_ref, w_ref, b_ref, s_ref, o_ref):
    z = jnp.dot(a_ref[...], w_ref[0, 0].astype(BF16), preferred_element_type=F32)
    o_ref[...] = ((z + b_ref[0, 0]) * s_ref[0]).astype(o_ref.dtype)


def _pool_mix(pooled, w, b, scale, layer):
    rows, width = pooled.shape
    ngroups, gw = w.shape[1], w.shape[2]
    tm = min(2 * ROW_TILE, rows)
    return pl.pallas_call(
        _pool_mix_kernel,
        out_shape=jax.ShapeDtypeStruct((rows, width), BF16),
        grid=(ngroups, rows // tm),
        in_specs=[
            pl.BlockSpec((tm, gw), lambda g, i: (i, g)),
            pl.BlockSpec((1, 1, gw, gw), lambda g, i: (layer, g, 0, 0)),
            pl.BlockSpec((1, 1, 1, gw), lambda g, i: (layer, g, 0, 0)),
            pl.BlockSpec((1, 1, gw), lambda g, i: (layer, 0, g)),
        ],
        out_specs=pl.BlockSpec((tm, gw), lambda g, i: (i, g)),
        compiler_params=_params("parallel", "parallel"),
    )(pooled, w, b, scale)


def _merge_kernel(a1_ref, w1_hbm, a2_ref, w2_hbm, g1_ref, g2_ref, o_ref, *scratch, layer):
    stream1 = _grid_stream(w1_hbm, scratch[:3], layer)
    stream2 = _grid_stream(w2_hbm, scratch[3:], layer)
    w1, w2 = stream1.begin(), stream2.begin()

    def block(cols):
        z1 = jnp.dot(a1_ref[...], w1[:, cols], preferred_element_type=F32)
        z2 = jnp.dot(a2_ref[...], w2[:, cols], preferred_element_type=F32)
        o_ref[:, cols] = (g1_ref[:, cols].astype(F32) * z1 + g2_ref[:, cols].astype(F32) * z2).astype(o_ref.dtype)

    _for_col_blocks(o_ref.shape[1], block)
    stream1.finish()
    stream2.finish()


def _merge(o_ssm, w_ssm_out, o_pool, w_pool_out, gates, layer):
    rows, k1 = o_ssm.shape
    k2 = o_pool.shape[1]
    n = w_ssm_out.shape[2]
    tm, tn = min(ROW_TILE, rows), min(COL_TILE, n)
    return pl.pallas_call(
        functools.partial(_merge_kernel, layer=layer),
        out_shape=jax.ShapeDtypeStruct((rows, n), BF16),
        grid=(n // tn, rows // tm),
        in_specs=[
            pl.BlockSpec((tm, k1), lambda j, i: (i, 0)),
            HBM_SPEC,
            pl.BlockSpec((tm, k2), lambda j, i: (i, 0)),
            HBM_SPEC,
            pl.BlockSpec((tm, tn), lambda j, i: (i, j)),
            pl.BlockSpec((tm, tn), lambda j, i: (i, n // tn + j)),
        ],
        out_specs=pl.BlockSpec((tm, tn), lambda j, i: (i, j)),
        scratch_shapes=_stream_scratch(k1, tn, rows // tm) + _stream_scratch(k2, tn, rows // tm),
        compiler_params=_params("arbitrary", "arbitrary"),
    )(o_ssm, w_ssm_out, o_pool, w_pool_out, gates, gates)


def _residual_kernel(a_ref, w_hbm, h_ref, g_ref, o_ref, *scratch, layer):
    stream = _grid_stream(w_hbm, scratch, layer)
    w = stream.begin()

    def block(cols):
        z = jnp.dot(a_ref[...], w[:, cols], preferred_element_type=F32)
        o_ref[:, cols] = h_ref[:, cols] + g_ref[0, :, cols] * z

    _for_col_blocks(o_ref.shape[1], block)
    stream.finish()


def _residual(a, w, layer, h, gate, *, tm, tn):
    rows, k = a.shape
    n = w.shape[2]
    rows_per_seq = rows // gate.shape[0]
    tm, tn = min(tm, rows_per_seq), min(tn, n)
    return pl.pallas_call(
        functools.partial(_residual_kernel, layer=layer),
        out_shape=jax.ShapeDtypeStruct((rows, n), F32),
        grid=(n // tn, rows // tm),
        in_specs=[
            pl.BlockSpec((tm, k), lambda j, i: (i, 0)),
            HBM_SPEC,
            pl.BlockSpec((tm, tn), lambda j, i: (i, j)),
            pl.BlockSpec((1, 1, tn), lambda j, i: (i * tm // rows_per_seq, 0, j)),
        ],
        out_specs=pl.BlockSpec((tm, tn), lambda j, i: (i, j)),
        scratch_shapes=_stream_scratch(k, tn, rows // tm),
        input_output_aliases={2: 0},
        compiler_params=_params("arbitrary", "arbitrary"),
    )(a, w, h, gate)


def _up_conv_kernel(a_ref, w_hbm, cwg_ref, cwv_ref, cbg_ref, cbv_ref, o_ref, carry_ref, *scratch, layer):
    _, _, cm, d = a_ref.shape
    tn = o_ref.shape[-1]
    seq_start = pl.program_id(2) == 0
    common = dict(layer=layer, panel=pl.program_id(0), n_panels=pl.num_programs(0),
                  step=pl.program_id(1) * pl.num_programs(2) + pl.program_id(2))
    gate_stream = _WeightStream(w_hbm, *scratch[:3], col_block0=0, **common)
    val_stream = _WeightStream(w_hbm, *scratch[3:], col_block0=pl.num_programs(0), **common)
    wg, wv = gate_stream.begin(), val_stream.begin()

    @pl.when(seq_start)
    def _():
        carry_ref[...] = jnp.zeros(carry_ref.shape, carry_ref.dtype)

    a = a_ref[0].reshape(CHUNK * cm, d)

    products = {}

    def matmuls(cols):
        for name, w in (("gate", wg), ("val", wv)):
            products[name, cols.start] = jnp.dot(a, w[:, cols], preferred_element_type=F32)

    _for_col_blocks(tn, matmuls)

    def conv(name, cw_ref, cb_ref, slot, cols):
        sub = cols.stop - cols.start
        first_row = lax.broadcasted_iota(jnp.int32, (cm, sub), 0) == 0
        r = products[name, cols.start].reshape(CHUNK, cm, sub)
        prev = []
        for t in range(CONV_WIDTH - 1):
            plane = r[CHUNK - (CONV_WIDTH - 1) + t]
            prev.append(jnp.where(first_row, carry_ref[slot + t, 0:1, cols], pltpu.roll(plane, 1, 0)))
            carry_ref[slot + t, :, cols] = jnp.broadcast_to(plane[cm - 1:cm, :], (SUBLANES, sub))
        planes = prev + [r[j] for j in range(CHUNK)]
        outs = []
        for j in range(CHUNK):
            acc = cb_ref[0, :, cols]
            for t in range(CONV_WIDTH):
                acc = acc + cw_ref[0, t:t + 1, cols] * planes[j + t]
            outs.append(acc)
        return outs

    def block(cols):
        gate = conv("gate", cwg_ref, cbg_ref, 0, cols)
        val = conv("val", cwv_ref, cbv_ref, CONV_WIDTH - 1, cols)
        for j in range(CHUNK):
            o_ref[0, j, :, cols] = (jax.nn.silu(gate[j]) * val[j]).astype(o_ref.dtype)

    _for_col_blocks(tn, block)
    gate_stream.finish()
    val_stream.finish()


def _up_conv(y4, w_up, conv_w, conv_b, layer):
    bsz, _, nchunk, d = y4.shape
    d_ff = w_up.shape[2] // 2
    cm, tn = min(ROW_TILE // CHUNK, nchunk), min(COL_TILE // 2, d_ff)
    nblk = d_ff // tn
    steps = bsz * (nchunk // cm)
    return pl.pallas_call(
        functools.partial(_up_conv_kernel, layer=layer),
        out_shape=jax.ShapeDtypeStruct((bsz, CHUNK, nchunk, d_ff), BF16),
        grid=(nblk, bsz, nchunk // cm),
        in_specs=[
            pl.BlockSpec((1, CHUNK, cm, d), lambda n, b, c: (b, 0, c, 0)),
            HBM_SPEC,
            pl.BlockSpec((1, CONV_WIDTH, tn), lambda n, b, c: (layer, 0, n)),
            pl.BlockSpec((1, CONV_WIDTH, tn), lambda n, b, c: (layer, 0, nblk + n)),
            pl.BlockSpec((1, 1, tn), lambda n, b, c: (layer, 0, n)),
            pl.BlockSpec((1, 1, tn), lambda n, b, c: (layer, 0, nblk + n)),
        ],
        out_specs=pl.BlockSpec((1, CHUNK, cm, tn), lambda n, b, c: (b, 0, c, n)),
        scratch_shapes=([pltpu.VMEM((2 * (CONV_WIDTH - 1), SUBLANES, tn), F32)]
                        + _stream_scratch(d, tn, steps) + _stream_scratch(d, tn, steps)),
        compiler_params=_params("arbitrary", "arbitrary", "arbitrary"),
    )(y4, w_up, conv_w, conv_w, conv_b, conv_b)


def kernel(x, c, w_cond, b_cond, ada_table, norm1_g, norm2_g, w_in, ssm_a_re, ssm_a_im, ssm_log_dt, ssm_b_re, ssm_b_im, ssm_c_re, ssm_c_im, ssm_d, w_glu, b_glu, w_pool, b_pool, pool_scale, w_ssm_out, w_pool_out, w_o, w_up, conv_w, conv_b, w_down, final_g):
    bsz, seq, d = x.shape
    depth = w_in.shape[0]
    nchunk = seq // CHUNK
    rows = bsz * seq
    ssm_width = w_glu.shape[2]
    pool_width = pool_scale.shape[1]
    assert seq % CHUNK == 0 and ssm_width % SLICE_CH == 0

    c_pad = jnp.pad(c, ((0, SUBLANES - bsz), (0, 0)))
    mods = _cond(c_pad, w_cond, b_cond, ada_table)[:, :bsz].reshape(depth, bsz, N_MOD, 1, d)
    kt, p, q, dre, dim = _s5_prep(ssm_a_re, ssm_a_im, ssm_log_dt, ssm_b_re, ssm_b_im, ssm_c_re, ssm_c_im)
    d_skip = ssm_d.reshape(depth, 1, ssm_width)
    b_glu = b_glu.reshape(depth, 1, ssm_width)
    b_pool = b_pool[:, :, None, :]
    pool_scale = pool_scale.reshape(depth, 1, pool_width)
    conv_b = conv_b.reshape(depth, 1, -1)

    h = None
    for l in range(depth):
        shift1, scale1, gate1, shift2, scale2, gate2 = (mods[l, :, i] for i in range(N_MOD))
        if l == 0:
            y, h = _norm(x, norm1_g[l], scale1, shift1, in_tokens=True, copy_out=True)
            h = h.reshape(rows, d)
        else:
            y = _norm(h.reshape(bsz, CHUNK, nchunk, d), norm1_g[l], scale1, shift1)
        y = y.reshape(rows, d)
        u = _proj(y, w_in, l, 0, ssm_width + pool_width, sigmoid=False)
        gates = _proj(y, w_in, l, ssm_width + pool_width, 2 * d, sigmoid=True)
        u4 = u.reshape(bsz, CHUNK, nchunk, -1)
        y_ssm = _s5(u4, kt, p, q, dre, dim, d_skip, l).reshape(rows, ssm_width)
        o_ssm = _glu(y_ssm, w_glu, b_glu, l)
        pooled = _pool(u4, ssm_width, pool_width).reshape(rows, pool_width)
        o_pool = _pool_mix(pooled, w_pool, b_pool, pool_scale, l)
        merged = _merge(o_ssm, w_ssm_out, o_pool, w_pool_out, gates, l)
        h = _residual(merged, w_o, l, h, gate1, tm=ROW_TILE, tn=COL_TILE)

        y = _norm(h.reshape(bsz, CHUNK, nchunk, d), norm2_g[l], scale2, shift2)
        act = _up_conv(y, w_up, conv_w, conv_b, l).reshape(rows, -1)
        h = _residual(act, w_down, l, h, gate2, tm=ROW_TILE // 2, tn=COL_TILE // 2)

    return _norm(h.reshape(bsz, CHUNK, nchunk, d), final_g, None, None, out_tokens=True, out_dtype=x.dtype)
```

```python
import functools

import jax
import jax.numpy as jnp
from jax import lax
from jax.experimental import pallas as pl
from jax.experimental.pallas import tpu as pltpu

F32 = jnp.float32
BF16 = jnp.bfloat16

CHUNK = 16
SSM_GROUP = 16
SSM_STATE = 64
SLICE_CH = 128
SLICE_GROUPS = SLICE_CH // SSM_GROUP
SLICE_STATES = SLICE_GROUPS * SSM_STATE
POOL_WINDOWS = (2, 4, 8, 16)
CONV_WIDTH = 3
N_MOD = 6
RMS_EPS = 1e-6
LAM_RE_MAX = -1e-4
SUBLANES = 8
LANES = 128
MXU_COLS = 256
V7X_VMEM_LIMIT = 56 * 1024 * 1024
ROW_TILE = 1024
COL_TILE = 1024
MAX_WEIGHT_CHUNKS = 8


def _params(*sem):
    return pltpu.CompilerParams(dimension_semantics=sem, vmem_limit_bytes=V7X_VMEM_LIMIT)


def _shift_rows(x, sh):
    rolled = pltpu.roll(x, sh, 0)
    idx = lax.broadcasted_iota(jnp.int32, x.shape, 0)
    return jnp.where(idx < sh, jnp.zeros_like(x), rolled)


class _WeightStream:
    def __init__(self, w_hbm, wbf_ref, stage_ref, sem, *, layer, col_block0, panel, n_panels, step, steps):
        self.w_hbm, self.wbf, self.stage, self.sem = w_hbm, wbf_ref, stage_ref, sem
        self.layer, self.col_block0 = layer, col_block0
        self.panel, self.step = panel, step
        _, self.k, self.tn = wbf_ref.shape
        self.kc = stage_ref.shape[1]
        self.n_chunks = self.k // self.kc
        self.lag = 1 if self.n_chunks + 1 <= steps else 0
        assert self.n_chunks + self.lag <= steps
        has_next = panel + 1 < n_panels
        self.starting = jnp.logical_and(has_next, step < self.n_chunks)
        self.finishing = jnp.logical_and(has_next, jnp.logical_and(step >= self.lag, step < self.n_chunks + self.lag))

    def _rows(self, chunk):
        start = chunk * self.kc
        return pl.ds(start if isinstance(start, int) else pl.multiple_of(start, self.kc), self.kc)

    def _copy(self, panel, chunk, slot):
        col = (self.col_block0 + panel) * self.tn
        cols = pl.ds(col if isinstance(col, int) else pl.multiple_of(col, self.tn), self.tn)
        return pltpu.make_async_copy(self.w_hbm.at[self.layer, self._rows(chunk), cols],
                                     self.stage.at[slot], self.sem.at[slot])

    def _cast(self, wbf_slot, chunk, slot):
        self.wbf[wbf_slot, self._rows(chunk), :] = self.stage[slot].astype(BF16)

    def begin(self):
        @pl.when(jnp.logical_and(self.panel == 0, self.step == 0))
        def _():
            self._copy(0, 0, 0).start()
            for c in range(self.n_chunks):
                if c + 1 < self.n_chunks:
                    self._copy(0, c + 1, (c + 1) % 2).start()
                self._copy(0, c, c % 2).wait()
                self._cast(0, c, c % 2)

        @pl.when(self.starting)
        def _():
            self._copy(self.panel + 1, self.step, self.step % 2).start()

        return self.wbf.at[self.panel % 2]

    def finish(self):
        @pl.when(self.finishing)
        def _():
            chunk = self.step - self.lag
            self._copy(self.panel + 1, chunk, chunk % 2).wait()
            self._cast((self.panel + 1) % 2, chunk, chunk % 2)


def _stream_scratch(k, tn, steps):
    n_chunks = 1
    while 2 * n_chunks <= min(steps - 1, MAX_WEIGHT_CHUNKS):
        n_chunks *= 2
    return [pltpu.VMEM((2, k, tn), BF16), pltpu.VMEM((2, k // n_chunks, tn), F32), pltpu.SemaphoreType.DMA((2,))]


HBM_SPEC = pl.BlockSpec(memory_space=pl.ANY)


def _for_col_blocks(width, body):
    sub = min(MXU_COLS, width)
    for j in range(width // sub):
        body(slice(j * sub, (j + 1) * sub))


def _cond_kernel(c_ref, w_ref, b_ref, ada_ref, o_ref):
    a = jax.nn.silu(c_ref[...]).astype(BF16)
    z = jnp.dot(a, w_ref[...].astype(BF16), preferred_element_type=F32) + b_ref[...]
    o_ref[...] = z[None] + ada_ref[...]


def _cond(c_pad, w_cond, b_cond, ada_table):
    rows, d = c_pad.shape
    n = w_cond.shape[1]
    depth = ada_table.shape[0]
    tn = min(COL_TILE // 2, n)
    return pl.pallas_call(
        _cond_kernel,
        out_shape=jax.ShapeDtypeStruct((depth, rows, n), F32),
        grid=(n // tn,),
        in_specs=[
            pl.BlockSpec((rows, d), lambda i: (0, 0)),
            pl.BlockSpec((d, tn), lambda i: (0, i)),
            pl.BlockSpec((1, tn), lambda i: (0, i)),
            pl.BlockSpec((depth, 1, tn), lambda i: (0, 0, i)),
        ],
        out_specs=pl.BlockSpec((depth, rows, tn), lambda i: (0, 0, i)),
        compiler_params=_params("arbitrary"),
    )(c_pad, w_cond, b_cond.reshape(1, n), ada_table.reshape(depth, 1, n))


def _norm_kernel(*refs, modulated, in_tokens, out_tokens, copy_out):
    refs = list(refs)
    x_ref, g_ref = refs[:2]
    scale_ref, shift_ref = refs[2:4] if modulated else (None, None)
    outs = refs[4 if modulated else 2:]
    o_ref = outs[0]
    copy_ref = outs[1] if copy_out else None
    stage_ref = outs[-1] if (in_tokens or out_tokens) else None
    tc = (x_ref.shape[1] // CHUNK) if in_tokens else x_ref.shape[2]
    lane_tiles = x_ref.shape[-1] // LANES
    plane_rows = lambda j: pl.ds(j, tc, stride=CHUNK)
    if in_tokens:
        for k in range(lane_tiles):
            stage_ref[k] = x_ref[0, :, k * LANES:(k + 1) * LANES]
    for j in range(CHUNK):
        if in_tokens:
            x = jnp.concatenate([stage_ref[k, plane_rows(j), :] for k in range(lane_tiles)], axis=-1)
        else:
            x = x_ref[0, j]
        y = x * lax.rsqrt(jnp.mean(x * x, axis=-1, keepdims=True) + RMS_EPS) * g_ref[...]
        if modulated:
            y = y * (1.0 + scale_ref[0]) + shift_ref[0]
        if out_tokens:
            for k in range(lane_tiles):
                stage_ref[k, plane_rows(j), :] = y[:, k * LANES:(k + 1) * LANES]
        else:
            o_ref[0, j] = y.astype(o_ref.dtype)
        if copy_out:
            copy_ref[0, j] = x
    if out_tokens:
        for k in range(lane_tiles):
            o_ref[0, :, k * LANES:(k + 1) * LANES] = stage_ref[k].astype(o_ref.dtype)


def _norm(x, g, scale, shift, *, in_tokens=False, out_tokens=False, copy_out=False, out_dtype=BF16):
    if in_tokens:
        bsz, seq, d = x.shape
        nchunk = seq // CHUNK
    else:
        bsz, _, nchunk, d = x.shape
    tc = min(16, nchunk)
    tok_spec = pl.BlockSpec((1, tc * CHUNK, d), lambda b, c: (b, c, 0))
    pos_spec = pl.BlockSpec((1, CHUNK, tc, d), lambda b, c: (b, 0, c, 0))
    modulated = scale is not None
    in_specs = [tok_spec if in_tokens else pos_spec, pl.BlockSpec((1, d), lambda b, c: (0, 0))]
    args = [x, g.reshape(1, d)]
    if modulated:
        mod_spec = pl.BlockSpec((1, 1, d), lambda b, c: (b, 0, 0))
        in_specs += [mod_spec, mod_spec]
        args += [scale, shift]
    pos_shape = (bsz, CHUNK, nchunk, d)
    out_shape = [jax.ShapeDtypeStruct((bsz, nchunk * CHUNK, d) if out_tokens else pos_shape, out_dtype)]
    out_specs = [tok_spec if out_tokens else pos_spec]
    if copy_out:
        out_shape.append(jax.ShapeDtypeStruct(pos_shape, F32))
        out_specs.append(pos_spec)
    staged = in_tokens or out_tokens
    scratch = [pltpu.VMEM((d // LANES, tc * CHUNK, LANES), F32)] if staged else []
    outs = pl.pallas_call(
        functools.partial(_norm_kernel, modulated=modulated, in_tokens=in_tokens, out_tokens=out_tokens,
                          copy_out=copy_out),
        out_shape=out_shape,
        grid=(bsz, nchunk // tc),
        in_specs=in_specs,
        out_specs=out_specs,
        scratch_shapes=scratch,
        compiler_params=_params("parallel", "parallel"),
    )(*args)
    return outs if copy_out else outs[0]


def _grid_stream(w_hbm, scratch, layer, steps, col_block0=0):
    return _WeightStream(w_hbm, *scratch, layer=layer, col_block0=col_block0, panel=pl.program_id(0),
                         n_panels=pl.num_programs(0), step=pl.program_id(1), steps=steps)


def _proj_kernel(a_ref, w_hbm, o_ref, *scratch, sigmoid, layer, steps, col_block0):
    stream = _grid_stream(w_hbm, scratch, layer, steps, col_block0)
    w = stream.begin()

    def block(cols):
        z = jnp.dot(a_ref[...], w[:, cols], preferred_element_type=F32)
        o_ref[:, cols] = (jax.nn.sigmoid(z) if sigmoid else z).astype(o_ref.dtype)

    _for_col_blocks(o_ref.shape[1], block)
    stream.finish()


def _proj(a, w, layer, col0, n, *, sigmoid):
    rows, k = a.shape
    tm, tn = min(ROW_TILE, rows), min(COL_TILE, n)
    return pl.pallas_call(
        functools.partial(_proj_kernel, sigmoid=sigmoid, layer=layer, steps=rows // tm, col_block0=col0 // tn),
        out_shape=jax.ShapeDtypeStruct((rows, n), BF16),
        grid=(n // tn, rows // tm),
        in_specs=[pl.BlockSpec((tm, k), lambda j, i: (i, 0)), HBM_SPEC],
        out_specs=pl.BlockSpec((tm, tn), lambda j, i: (i, j)),
        scratch_shapes=_stream_scratch(k, tn, rows // tm),
        compiler_params=_params("arbitrary", "arbitrary"),
    )(a, w)


def _discretise(a_re, a_im, log_dt):
    lam_re = jnp.minimum(a_re, LAM_RE_MAX)
    lam_im = a_im
    dt = jnp.exp(log_dt)
    mag = jnp.exp(lam_re * dt)
    abar_re = mag * jnp.cos(lam_im * dt)
    abar_im = mag * jnp.sin(lam_im * dt)
    den = lam_re * lam_re + lam_im * lam_im
    x_re = abar_re - 1.0
    f_re = (x_re * lam_re + abar_im * lam_im) / den
    f_im = (abar_im * lam_re - x_re * lam_im) / den
    return abar_re, abar_im, f_re, f_im


def _split_bf16(x):
    hi = x.astype(BF16)
    return hi, (x - hi.astype(F32)).astype(BF16)


def _dot_split(a, b):
    a_hi, a_lo = a
    b_hi, b_lo = b
    dot = functools.partial(jnp.dot, preferred_element_type=F32)
    return dot(a_hi, b_hi) + (dot(a_hi, b_lo) + dot(a_lo, b_hi))


def _s5_prep_kernel(rowp_ref, colp_ref, bt_ref, ct_ref, kt_ref, p_ref, q_ref, dre_ref, dim_ref):
    ns, ch = SLICE_STATES, SLICE_CH
    ar, ai, f_re, f_im = _discretise(rowp_ref[0, 0, 0:1, :], rowp_ref[0, 0, 1:2, :], rowp_ref[0, 0, 2:3, :])
    bt_re = jnp.concatenate([bt_ref[0, 0, 0]] * SLICE_GROUPS, axis=0)
    bt_im = jnp.concatenate([bt_ref[0, 0, 1]] * SLICE_GROUPS, axis=0)
    same_group = (lax.broadcasted_iota(jnp.int32, (ch, ns), 0) // SSM_GROUP
                  == lax.broadcasted_iota(jnp.int32, (ch, ns), 1) // SSM_STATE)
    e_re = jnp.where(same_group, f_re * bt_re - f_im * bt_im, 0.0)
    e_im = jnp.where(same_group, f_re * bt_im + f_im * bt_re, 0.0)
    same_group_t = (lax.broadcasted_iota(jnp.int32, (ns, ch), 0) // SSM_STATE
                    == lax.broadcasted_iota(jnp.int32, (ns, ch), 1) // SSM_GROUP)
    c_re = jnp.where(same_group_t, ct_ref[0, 0, 0], 0.0)
    c_im = jnp.where(same_group_t, ct_ref[0, 0, 1], 0.0)
    c_re_split, c_im_split = _split_bf16(c_re), _split_bf16(c_im)
    for t in range(CHUNK):
        j = CHUNK - 1 - t
        e_re_split, e_im_split = _split_bf16(e_re), _split_bf16(e_im)
        p_ref[0, 0, j * ch:(j + 1) * ch, 0:ns] = e_re_split[0]
        p_ref[0, 0, j * ch:(j + 1) * ch, ns:2 * ns] = e_im_split[0]
        kt_ref[0, 0, t] = (_dot_split(e_re_split, c_re_split) - _dot_split(e_im_split, c_im_split)).astype(BF16)
        e_re, e_im = e_re * ar - e_im * ai, e_re * ai + e_im * ar

    d_re, d_im = ar, ai
    for _ in range(CHUNK.bit_length() - 1):
        d_re, d_im = d_re * d_re - d_im * d_im, 2.0 * d_re * d_im
    p_re, p_im = d_re, d_im
    for r in range(SUBLANES):
        dre_ref[0, 0, r:r + 1, :] = p_re
        dim_ref[0, 0, r:r + 1, :] = p_im
        p_re, p_im = p_re * d_re - p_im * d_im, p_re * d_im + p_im * d_re

    arc, aic, _, _ = _discretise(colp_ref[0, 0, 0], colp_ref[0, 0, 1], colp_ref[0, 0, 2])
    w_re, w_im = c_re, c_im
    for i in range(CHUNK):
        w_re, w_im = w_re * arc - w_im * aic, w_re * aic + w_im * arc
        q_ref[0, 0, 0:ns, i * ch:(i + 1) * ch] = w_re.astype(BF16)
        q_ref[0, 0, ns:2 * ns, i * ch:(i + 1) * ch] = (-w_im).astype(BF16)


def _s5_prep(a_re, a_im, log_dt, b_re, b_im, c_re, c_im):
    depth, groups, nst = a_re.shape
    nsl = groups // SLICE_GROUPS
    ns, ch = SLICE_STATES, SLICE_CH
    dt_b = jnp.broadcast_to(log_dt[:, :, None], a_re.shape)
    rows = jnp.stack([a_re, a_im, dt_b], axis=2).reshape(depth, nsl, SLICE_GROUPS, 3, nst)
    rows = jnp.swapaxes(rows, 2, 3).reshape(depth, nsl, 3, ns)
    rowp = jnp.pad(rows, ((0, 0), (0, 0), (0, SUBLANES - 3), (0, 0)))
    colp = jnp.broadcast_to(rows[..., None], (depth, nsl, 3, ns, ch))
    bt = jnp.stack([b_re, b_im], axis=1).reshape(depth, 2, nsl, ns, SSM_GROUP)
    bt = jnp.transpose(bt, (0, 2, 1, 4, 3))
    ct = jnp.stack([c_re, c_im], axis=1).reshape(depth, 2, nsl, SLICE_GROUPS, SSM_GROUP, nst)
    ct = jnp.transpose(ct, (0, 2, 1, 3, 5, 4)).reshape(depth, nsl, 2, ns, SSM_GROUP)
    ct = jnp.tile(ct, (1, 1, 1, 1, SLICE_GROUPS))
    out_shapes = (
        jax.ShapeDtypeStruct((depth, nsl, CHUNK, ch, ch), BF16),
        jax.ShapeDtypeStruct((depth, nsl, CHUNK * ch, 2 * ns), BF16),
        jax.ShapeDtypeStruct((depth, nsl, 2 * ns, CHUNK * ch), BF16),
        jax.ShapeDtypeStruct((depth, nsl, SUBLANES, ns), F32),
        jax.ShapeDtypeStruct((depth, nsl, SUBLANES, ns), F32),
    )
    return pl.pallas_call(
        _s5_prep_kernel,
        out_shape=out_shapes,
        grid=(depth, nsl),
        in_specs=[
            pl.BlockSpec((1, 1, SUBLANES, ns), lambda l, s: (l, s, 0, 0)),
            pl.BlockSpec((1, 1, 3, ns, ch), lambda l, s: (l, s, 0, 0, 0)),
            pl.BlockSpec((1, 1, 2, SSM_GROUP, ns), lambda l, s: (l, s, 0, 0, 0)),
            pl.BlockSpec((1, 1, 2, ns, ch), lambda l, s: (l, s, 0, 0, 0)),
        ],
        out_specs=(
            pl.BlockSpec((1, 1, CHUNK, ch, ch), lambda l, s: (l, s, 0, 0, 0)),
            pl.BlockSpec((1, 1, CHUNK * ch, 2 * ns), lambda l, s: (l, s, 0, 0)),
            pl.BlockSpec((1, 1, 2 * ns, CHUNK * ch), lambda l, s: (l, s, 0, 0)),
            pl.BlockSpec((1, 1, SUBLANES, ns), lambda l, s: (l, s, 0, 0)),
            pl.BlockSpec((1, 1, SUBLANES, ns), lambda l, s: (l, s, 0, 0)),
        ),
        compiler_params=_params("parallel", "parallel"),
    )(rowp, colp, bt, ct)


def _chunk_scan(x_re, x_im, dre_ref, dim_ref):
    nrows = x_re.shape[0]
    row_in_block = lax.broadcasted_iota(jnp.int32, x_re.shape, 0) % SUBLANES
    sh = 1
    while sh < SUBLANES:
        d_re, d_im = dre_ref[0, 0, sh - 1:sh, :], dim_ref[0, 0, sh - 1:sh, :]
        keep = row_in_block >= sh
        s_re = jnp.where(keep, pltpu.roll(x_re, sh, 0), 0.0)
        s_im = jnp.where(keep, pltpu.roll(x_im, sh, 0), 0.0)
        x_re, x_im = x_re + d_re * s_re - d_im * s_im, x_im + d_re * s_im + d_im * s_re
        sh *= 2
    pw_re, pw_im = dre_ref[0, 0], dim_ref[0, 0]
    out_re, out_im = [x_re[0:SUBLANES]], [x_im[0:SUBLANES]]
    for blk in range(1, nrows // SUBLANES):
        rows = slice(blk * SUBLANES, (blk + 1) * SUBLANES)
        c_re = jnp.broadcast_to(out_re[-1][SUBLANES - 1:SUBLANES], pw_re.shape)
        c_im = jnp.broadcast_to(out_im[-1][SUBLANES - 1:SUBLANES], pw_re.shape)
        out_re.append(x_re[rows] + pw_re * c_re - pw_im * c_im)
        out_im.append(x_im[rows] + pw_re * c_im + pw_im * c_re)
    return jnp.concatenate(out_re, axis=0), jnp.concatenate(out_im, axis=0)


def _s5_kernel(u_ref, kt_ref, p_ref, q_ref, dre_ref, dim_ref, dskip_ref, o_ref, toep_ref):
    ch, ns = SLICE_CH, SLICE_STATES
    per_blk = MXU_COLS // ch
    nblk = CHUNK // per_blk

    @pl.when(pl.program_id(1) == 0)
    def _():
        for i in range(CHUNK):
            last_j = (i // per_blk + 1) * per_blk
            for j in range(last_j):
                blk = kt_ref[0, 0, i - j] if j <= i else jnp.zeros((ch, ch), BF16)
                toep_ref[j * ch:(j + 1) * ch, i * ch:(i + 1) * ch] = blk

    u_all = jnp.concatenate([u_ref[0, j] for j in range(CHUNK)], axis=-1)
    x_loc = jnp.dot(u_all, p_ref[0, 0], preferred_element_type=F32)
    x_re, x_im = _chunk_scan(x_loc[:, :ns], x_loc[:, ns:], dre_ref, dim_ref)
    x_prev = jnp.concatenate([_shift_rows(x_re, 1), _shift_rows(x_im, 1)], axis=-1).astype(BF16)
    for t in range(nblk):
        cols = slice(t * MXU_COLS, (t + 1) * MXU_COLS)
        k_rows = (t + 1) * MXU_COLS
        y_blk = (jnp.dot(u_all[:, :k_rows], toep_ref[:k_rows, cols], preferred_element_type=F32)
                 + jnp.dot(x_prev, q_ref[0, 0, :, cols], preferred_element_type=F32))
        for s in range(per_blk):
            i = t * per_blk + s
            y = y_blk[:, s * ch:(s + 1) * ch] + u_ref[0, i].astype(F32) * dskip_ref[0]
            o_ref[0, i] = jax.nn.gelu(y).astype(o_ref.dtype)


def _s5(u4, kt, p, q, dre, dim, d_skip, layer):
    bsz, _, nchunk, _ = u4.shape
    ch, ns = SLICE_CH, SLICE_STATES
    nsl = kt.shape[1]
    assert nchunk % SUBLANES == 0
    return pl.pallas_call(
        _s5_kernel,
        out_shape=jax.ShapeDtypeStruct((bsz, CHUNK, nchunk, nsl * ch), BF16),
        grid=(nsl, bsz),
        in_specs=[
            pl.BlockSpec((1, CHUNK, nchunk, ch), lambda s, b: (b, 0, 0, s)),
            pl.BlockSpec((1, 1, CHUNK, ch, ch), lambda s, b: (layer, s, 0, 0, 0)),
            pl.BlockSpec((1, 1, CHUNK * ch, 2 * ns), lambda s, b: (layer, s, 0, 0)),
            pl.BlockSpec((1, 1, 2 * ns, CHUNK * ch), lambda s, b: (layer, s, 0, 0)),
            pl.BlockSpec((1, 1, SUBLANES, ns), lambda s, b: (layer, s, 0, 0)),
            pl.BlockSpec((1, 1, SUBLANES, ns), lambda s, b: (layer, s, 0, 0)),
            pl.BlockSpec((1, 1, ch), lambda s, b: (layer, 0, s)),
        ],
        out_specs=pl.BlockSpec((1, CHUNK, nchunk, ch), lambda s, b: (b, 0, 0, s)),
        scratch_shapes=[pltpu.VMEM((CHUNK * ch, CHUNK * ch), BF16)],
        compiler_params=_params("arbitrary", "arbitrary"),
    )(u4, kt, p, q, dre, dim, d_skip)


def _glu_kernel(a_ref, w_hbm, b_ref, y_ref, o_ref, *scratch, layer, steps):
    stream = _grid_stream(w_hbm, scratch, layer, steps)
    w = stream.begin()

    def block(cols):
        z = jnp.dot(a_ref[...], w[:, cols], preferred_element_type=F32) + b_ref[0, :, cols]
        o_ref[:, cols] = (y_ref[:, cols].astype(F32) * jax.nn.sigmoid(z)).astype(o_ref.dtype)

    _for_col_blocks(o_ref.shape[1], block)
    stream.finish()


def _glu(y, w, b, layer):
    rows, k = y.shape
    n = w.shape[2]
    tm, tn = min(ROW_TILE, rows), min(COL_TILE, n)
    return pl.pallas_call(
        functools.partial(_glu_kernel, layer=layer, steps=rows // tm),
        out_shape=jax.ShapeDtypeStruct((rows, n), BF16),
        grid=(n // tn, rows // tm),
        in_specs=[
            pl.BlockSpec((tm, k), lambda j, i: (i, 0)),
            HBM_SPEC,
            pl.BlockSpec((1, 1, tn), lambda j, i: (layer, 0, j)),
            pl.BlockSpec((tm, tn), lambda j, i: (i, j)),
        ],
        out_specs=pl.BlockSpec((tm, tn), lambda j, i: (i, j)),
        scratch_shapes=_stream_scratch(k, tn, rows // tm),
        compiler_params=_params("arbitrary", "arbitrary"),
    )(y, w, b, y)


def _pool_kernel(v_ref, o_ref, *, blocks_per_group):
    nchunk = v_ref.shape[2]
    group = pl.program_id(1) // blocks_per_group
    first_row = lax.broadcasted_iota(jnp.int32, (nchunk, 1), 0) == 0

    def pooled(win):
        v = [v_ref[0, j].astype(F32) for j in range(CHUNK)]
        prefix = [v[0]]
        for j in range(1, CHUNK):
            prefix.append(prefix[-1] + v[j])
        for j in range(CHUNK):
            if j >= win:
                wsum = prefix[j] - prefix[j - win]
            elif j == win - 1:
                wsum = prefix[j]
            else:
                wsum = prefix[j] + _shift_rows(prefix[CHUNK - 1] - prefix[j - win + CHUNK], 1)
            count = jnp.where(first_row, float(min(j + 1, win)), float(win))
            o_ref[0, j] = (wsum / count - v[j]).astype(o_ref.dtype)

    for g, win in enumerate(POOL_WINDOWS):
        pl.when(group == g)(functools.partial(pooled, win))


def _pool(u4, col0, width):
    bsz, _, nchunk, _ = u4.shape
    tn = 128
    blocks_per_group = width // len(POOL_WINDOWS) // tn
    return pl.pallas_call(
        functools.partial(_pool_kernel, blocks_per_group=blocks_per_group),
        out_shape=jax.ShapeDtypeStruct((bsz, CHUNK, nchunk, width), BF16),
        grid=(bsz, width // tn),
        in_specs=[pl.BlockSpec((1, CHUNK, nchunk, tn), lambda b, t: (b, 0, 0, col0 // tn + t))],
        out_specs=pl.BlockSpec((1, CHUNK, nchunk, tn), lambda b, t: (b, 0, 0, t)),
        compiler_params=_params("parallel", "parallel"),
    )(u4)


def _pool_mix_kernel(a_ref, w_ref, b_ref, s_ref, o_ref):
    z = jnp.dot(a_ref[...], w_ref[0, 0].astype(BF16), preferred_element_type=F32)
    o_ref[...] = ((z + b_ref[0, 0]) * s_ref[0]).astype(o_ref.dtype)


def _pool_mix(pooled, w, b, scale, layer):
    rows, width = pooled.shape
    ngroups, gw = w.shape[1], w.shape[2]
    tm = min(2 * ROW_TILE, rows)
    return pl.pallas_call(
        _pool_mix_kernel,
        out_shape=jax.ShapeDtypeStruct((rows, width), BF16),
        grid=(ngroups, rows // tm),
        in_specs=[
            pl.BlockSpec((tm, gw), lambda g, i: (i, g)),
            pl.BlockSpec((1, 1, gw, gw), lambda g, i: (layer, g, 0, 0)),
            pl.BlockSpec((1, 1, 1, gw), lambda g, i: (layer, g, 0, 0)),
            pl.BlockSpec((1, 1, gw), lambda g, i: (layer, 0, g)),
        ],
        out_specs=pl.BlockSpec((tm, gw), lambda g, i: (i, g)),
        compiler_params=_params("parallel", "parallel"),
    )(pooled, w, b, scale)


def _merge_kernel(a1_ref, w1_hbm, a2_ref, w2_hbm, g1_ref, g2_ref, o_ref, *scratch, layer, steps):
    stream1 = _grid_stream(w1_hbm, scratch[:3], layer, steps)
    stream2 = _grid_stream(w2_hbm, scratch[3:], layer, steps)
    w1, w2 = stream1.begin(), stream2.begin()

    def block(cols):
        z1 = jnp.dot(a1_ref[...], w1[:, cols], preferred_element_type=F32)
        z2 = jnp.dot(a2_ref[...], w2[:, cols], preferred_element_type=F32)
        o_ref[:, cols] = (g1_ref[:, cols].astype(F32) * z1 + g2_ref[:, cols].astype(F32) * z2).astype(o_ref.dtype)

    _for_col_blocks(o_ref.shape[1], block)
    stream1.finish()
    stream2.finish()


def _merge(o_ssm, w_ssm_out, o_pool, w_pool_out, gates, layer):
    rows, k1 = o_ssm.shape
    k2 = o_pool.shape[1]
    n = w_ssm_out.shape[2]
    tm, tn = min(ROW_TILE, rows), min(COL_TILE, n)
    return pl.pallas_call(
        functools.partial(_merge_kernel, layer=layer, steps=rows // tm),
        out_shape=jax.ShapeDtypeStruct((rows, n), BF16),
        grid=(n // tn, rows // tm),
        in_specs=[
            pl.BlockSpec((tm, k1), lambda j, i: (i, 0)),
            HBM_SPEC,
            pl.BlockSpec((tm, k2), lambda j, i: (i, 0)),
            HBM_SPEC,
            pl.BlockSpec((tm, tn), lambda j, i: (i, j)),
            pl.BlockSpec((tm, tn), lambda j, i: (i, n // tn + j)),
        ],
        out_specs=pl.BlockSpec((tm, tn), lambda j, i: (i, j)),
        scratch_shapes=_stream_scratch(k1, tn, rows // tm) + _stream_scratch(k2, tn, rows // tm),
        compiler_params=_params("arbitrary", "arbitrary"),
    )(o_ssm, w_ssm_out, o_pool, w_pool_out, gates, gates)


def _residual_kernel(a_ref, w_hbm, h_ref, g_ref, o_ref, *scratch, layer, steps):
    stream = _grid_stream(w_hbm, scratch, layer, steps)
    w = stream.begin()

    def block(cols):
        z = jnp.dot(a_ref[...], w[:, cols], preferred_element_type=F32)
        o_ref[:, cols] = h_ref[:, cols] + g_ref[0, :, cols] * z

    _for_col_blocks(o_ref.shape[1], block)
    stream.finish()


def _residual(a, w, layer, h, gate, *, tm, tn):
    rows, k = a.shape
    n = w.shape[2]
    rows_per_seq = rows // gate.shape[0]
    tm, tn = min(tm, rows_per_seq), min(tn, n)
    return pl.pallas_call(
        functools.partial(_residual_kernel, layer=layer, steps=rows // tm),
        out_shape=jax.ShapeDtypeStruct((rows, n), F32),
        grid=(n // tn, rows // tm),
        in_specs=[
            pl.BlockSpec((tm, k), lambda j, i: (i, 0)),
            HBM_SPEC,
            pl.BlockSpec((tm, tn), lambda j, i: (i, j)),
            pl.BlockSpec((1, 1, tn), lambda j, i: (i * tm // rows_per_seq, 0, j)),
        ],
        out_specs=pl.BlockSpec((tm, tn), lambda j, i: (i, j)),
        scratch_shapes=_stream_scratch(k, tn, rows // tm),
        input_output_aliases={2: 0},
        compiler_params=_params("arbitrary", "arbitrary"),
    )(a, w, h, gate)


def _up_conv_kernel(a_ref, w_hbm, cwg_ref, cwv_ref, cbg_ref, cbv_ref, o_ref, carry_ref, *scratch, layer, steps):
    _, _, cm, d = a_ref.shape
    tn = o_ref.shape[-1]
    seq_start = pl.program_id(2) == 0
    common = dict(layer=layer, panel=pl.program_id(0), n_panels=pl.num_programs(0),
                  step=pl.program_id(1) * pl.num_programs(2) + pl.program_id(2), steps=steps)
    gate_stream = _WeightStream(w_hbm, *scratch[:3], col_block0=0, **common)
    val_stream = _WeightStream(w_hbm, *scratch[3:], col_block0=pl.num_programs(0), **common)
    wg, wv = gate_stream.begin(), val_stream.begin()

    @pl.when(seq_start)
    def _():
        carry_ref[...] = jnp.zeros(carry_ref.shape, carry_ref.dtype)

    a = a_ref[0].reshape(CHUNK * cm, d)

    products = {}

    def matmuls(cols):
        for name, w in (("gate", wg), ("val", wv)):
            products[name, cols.start] = jnp.dot(a, w[:, cols], preferred_element_type=F32)

    _for_col_blocks(tn, matmuls)

    def conv(name, cw_ref, cb_ref, slot, cols):
        sub = cols.stop - cols.start
        first_row = lax.broadcasted_iota(jnp.int32, (cm, sub), 0) == 0
        r = products[name, cols.start].reshape(CHUNK, cm, sub)
        prev = []
        for t in range(CONV_WIDTH - 1):
            plane = r[CHUNK - (CONV_WIDTH - 1) + t]
            prev.append(jnp.where(first_row, carry_ref[slot + t, 0:1, cols], pltpu.roll(plane, 1, 0)))
            carry_ref[slot + t, :, cols] = jnp.broadcast_to(plane[cm - 1:cm, :], (SUBLANES, sub))
        planes = prev + [r[j] for j in range(CHUNK)]
        outs = []
        for j in range(CHUNK):
            acc = cb_ref[0, :, cols]
            for t in range(CONV_WIDTH):
                acc = acc + cw_ref[0, t:t + 1, cols] * planes[j + t]
            outs.append(acc)
        return outs

    def block(cols):
        gate = conv("gate", cwg_ref, cbg_ref, 0, cols)
        val = conv("val", cwv_ref, cbv_ref, CONV_WIDTH - 1, cols)
        for j in range(CHUNK):
            o_ref[0, j, :, cols] = (jax.nn.silu(gate[j]) * val[j]).astype(o_ref.dtype)

    _for_col_blocks(tn, block)
    gate_stream.finish()
    val_stream.finish()


def _up_conv(y4, w_up, conv_w, conv_b, layer):
    bsz, _, nchunk, d = y4.shape
    d_ff = w_up.shape[2] // 2
    cm, tn = min(ROW_TILE // CHUNK, nchunk), min(COL_TILE // 2, d_ff)
    nblk = d_ff // tn
    steps = bsz * (nchunk // cm)
    return pl.pallas_call(
        functools.partial(_up_conv_kernel, layer=layer, steps=steps),
        out_shape=jax.ShapeDtypeStruct((bsz, CHUNK, nchunk, d_ff), BF16),
        grid=(nblk, bsz, nchunk // cm),
        in_specs=[
            pl.BlockSpec((1, CHUNK, cm, d), lambda n, b, c: (b, 0, c, 0)),
            HBM_SPEC,
            pl.BlockSpec((1, CONV_WIDTH, tn), lambda n, b, c: (layer, 0, n)),
            pl.BlockSpec((1, CONV_WIDTH, tn), lambda n, b, c: (layer, 0, nblk + n)),
            pl.BlockSpec((1, 1, tn), lambda n, b, c: (layer, 0, n)),
            pl.BlockSpec((1, 1, tn), lambda n, b, c: (layer, 0, nblk + n)),
        ],
        out_specs=pl.BlockSpec((1, CHUNK, cm, tn), lambda n, b, c: (b, 0, c, n)),
        scratch_shapes=([pltpu.VMEM((2 * (CONV_WIDTH - 1), SUBLANES, tn), F32)]
                        + _stream_scratch(d, tn, steps) + _stream_scratch(d, tn, steps)),
        compiler_params=_params("arbitrary", "arbitrary", "arbitrary"),
    )(y4, w_up, conv_w, conv_w, conv_b, conv_b)


def kernel(x, c, w_cond, b_cond, ada_table, norm1_g, norm2_g, w_in, ssm_a_re, ssm_a_im, ssm_log_dt, ssm_b_re, ssm_b_im, ssm_c_re, ssm_c_im, ssm_d, w_glu, b_glu, w_pool, b_pool, pool_scale, w_ssm_out, w_pool_out, w_o, w_up, conv_w, conv_b, w_down, final_g):
    bsz, seq, d = x.shape
    depth = w_in.shape[0]
    nchunk = seq // CHUNK
    rows = bsz * seq
    ssm_width = w_glu.shape[2]
    pool_width = pool_scale.shape[1]
    assert seq % CHUNK == 0 and ssm_width % SLICE_CH == 0

    c_pad = jnp.pad(c, ((0, SUBLANES - bsz), (0, 0)))
    mods = _cond(c_pad, w_cond, b_cond, ada_table)[:, :bsz].reshape(depth, bsz, N_MOD, 1, d)
    kt, p, q, dre, dim = _s5_prep(ssm_a_re, ssm_a_im, ssm_log_dt, ssm_b_re, ssm_b_im, ssm_c_re, ssm_c_im)
    d_skip = ssm_d.reshape(depth, 1, ssm_width)
    b_glu = b_glu.reshape(depth, 1, ssm_width)
    b_pool = b_pool[:, :, None, :]
    pool_scale = pool_scale.reshape(depth, 1, pool_width)
    conv_b = conv_b.reshape(depth, 1, -1)

    h = None
    for l in range(depth):
        shift1, scale1, gate1, shift2, scale2, gate2 = (mods[l, :, i] for i in range(N_MOD))
        if l == 0:
            y, h = _norm(x, norm1_g[l], scale1, shift1, in_tokens=True, copy_out=True)
            h = h.reshape(rows, d)
        else:
            y = _norm(h.reshape(bsz, CHUNK, nchunk, d), norm1_g[l], scale1, shift1)
        y = y.reshape(rows, d)
        u = _proj(y, w_in, l, 0, ssm_width + pool_width, sigmoid=False)
        gates = _proj(y, w_in, l, ssm_width + pool_width, 2 * d, sigmoid=True)
        u4 = u.reshape(bsz, CHUNK, nchunk, -1)
        y_ssm = _s5(u4, kt, p, q, dre, dim, d_skip, l).reshape(rows, ssm_width)
        o_ssm = _glu(y_ssm, w_glu, b_glu, l)
        pooled = _pool(u4, ssm_width, pool_width).reshape(rows, pool_width)
        o_pool = _pool_mix(pooled, w_pool, b_pool, pool_scale, l)
        merged = _merge(o_ssm, w_ssm_out, o_pool, w_pool_out, gates, l)
        h = _residual(merged, w_o, l, h, gate1, tm=ROW_TILE, tn=COL_TILE)

        y = _norm(h.reshape(bsz, CHUNK, nchunk, d), norm2_g[l], scale2, shift2)
        act = _up_conv(y, w_up, conv_w, conv_b, l).reshape(rows, -1)
        h = _residual(act, w_down, l, h, gate2, tm=ROW_TILE // 2, tn=COL_TILE // 2)

    return _norm(h.reshape(bsz, CHUNK, nchunk, d), final_g, None, None, out_tokens=True, out_dtype=x.dtype)
```

```python
import functools

import jax
import jax.numpy as jnp
from jax import lax
from jax.experimental import pallas as pl
from jax.experimental.pallas import tpu as pltpu

F32 = jnp.float32
BF16 = jnp.bfloat16

CHUNK = 16
SSM_GROUP = 16
SSM_STATE = 64
SLICE_CH = 128
SLICE_GROUPS = SLICE_CH // SSM_GROUP
SLICE_STATES = SLICE_GROUPS * SSM_STATE
POOL_WINDOWS = (2, 4, 8, 16)
CONV_WIDTH = 3
N_MOD = 6
RMS_EPS = 1e-6
LAM_RE_MAX = -1e-4
SUBLANES = 8
LANES = 128
MXU_COLS = 256
V7X_VMEM_LIMIT = 56 * 1024 * 1024
ROW_TILE = 1024
COL_TILE = 1024
MAX_WEIGHT_CHUNKS = 8


def _params(*sem):
    return pltpu.CompilerParams(dimension_semantics=sem, vmem_limit_bytes=V7X_VMEM_LIMIT)


def _shift_rows(x, sh):
    rolled = pltpu.roll(x, sh, 0)
    idx = lax.broadcasted_iota(jnp.int32, x.shape, 0)
    return jnp.where(idx < sh, jnp.zeros_like(x), rolled)


class _WeightStream:
    def __init__(self, w_hbm, wbf_ref, stage_ref, sem, *, layer, col_block0, panel, n_panels, step, steps):
        self.w_hbm, self.wbf, self.stage, self.sem = w_hbm, wbf_ref, stage_ref, sem
        self.layer, self.col_block0 = layer, col_block0
        self.panel, self.step = panel, step
        _, self.k, self.tn = wbf_ref.shape
        self.kc = stage_ref.shape[1]
        self.n_chunks = self.k // self.kc
        self.lag = 1 if self.n_chunks + 1 <= steps else 0
        assert self.n_chunks + self.lag <= steps
        has_next = panel + 1 < n_panels
        self.starting = jnp.logical_and(has_next, step < self.n_chunks)
        self.finishing = jnp.logical_and(has_next, jnp.logical_and(step >= self.lag, step < self.n_chunks + self.lag))

    def _rows(self, chunk):
        start = chunk * self.kc
        return pl.ds(start if isinstance(start, int) else pl.multiple_of(start, self.kc), self.kc)

    def _copy(self, panel, chunk, slot):
        col = (self.col_block0 + panel) * self.tn
        cols = pl.ds(col if isinstance(col, int) else pl.multiple_of(col, self.tn), self.tn)
        return pltpu.make_async_copy(self.w_hbm.at[self.layer, self._rows(chunk), cols],
                                     self.stage.at[slot], self.sem.at[slot])

    def _cast(self, wbf_slot, chunk, slot):
        self.wbf[wbf_slot, self._rows(chunk), :] = self.stage[slot].astype(BF16)

    def begin(self):
        @pl.when(jnp.logical_and(self.panel == 0, self.step == 0))
        def _():
            self._copy(0, 0, 0).start()
            for c in range(self.n_chunks):
                if c + 1 < self.n_chunks:
                    self._copy(0, c + 1, (c + 1) % 2).start()
                self._copy(0, c, c % 2).wait()
                self._cast(0, c, c % 2)

        @pl.when(self.starting)
        def _():
            self._copy(self.panel + 1, self.step, self.step % 2).start()

        return self.wbf.at[self.panel % 2]

    def finish(self):
        @pl.when(self.finishing)
        def _():
            chunk = self.step - self.lag
            self._copy(self.panel + 1, chunk, chunk % 2).wait()
            self._cast((self.panel + 1) % 2, chunk, chunk % 2)


def _stream_scratch(k, tn, steps):
    n_chunks = 1
    while 2 * n_chunks <= min(steps - 1, MAX_WEIGHT_CHUNKS):
        n_chunks *= 2
    return [pltpu.VMEM((2, k, tn), BF16), pltpu.VMEM((2, k // n_chunks, tn), F32), pltpu.SemaphoreType.DMA((2,))]


HBM_SPEC = pl.BlockSpec(memory_space=pl.ANY)


def _sigmoid(x):
    return 0.5 * jnp.tanh(0.5 * x) + 0.5


def _silu(x):
    h = 0.5 * x
    return h * jnp.tanh(h) + h


def _for_col_blocks(width, body):
    sub = min(MXU_COLS, width)
    for j in range(width // sub):
        body(slice(j * sub, (j + 1) * sub))


def _cond_kernel(c_ref, w_ref, b_ref, ada_ref, o_ref):
    a = jax.nn.silu(c_ref[...]).astype(BF16)
    z = jnp.dot(a, w_ref[...].astype(BF16), preferred_element_type=F32) + b_ref[...]
    o_ref[...] = z[None] + ada_ref[...]


def _cond(c_pad, w_cond, b_cond, ada_table):
    rows, d = c_pad.shape
    n = w_cond.shape[1]
    depth = ada_table.shape[0]
    tn = min(COL_TILE // 2, n)
    return pl.pallas_call(
        _cond_kernel,
        out_shape=jax.ShapeDtypeStruct((depth, rows, n), F32),
        grid=(n // tn,),
        in_specs=[
            pl.BlockSpec((rows, d), lambda i: (0, 0)),
            pl.BlockSpec((d, tn), lambda i: (0, i)),
            pl.BlockSpec((1, tn), lambda i: (0, i)),
            pl.BlockSpec((depth, 1, tn), lambda i: (0, 0, i)),
        ],
        out_specs=pl.BlockSpec((depth, rows, tn), lambda i: (0, 0, i)),
        compiler_params=_params("arbitrary"),
    )(c_pad, w_cond, b_cond.reshape(1, n), ada_table.reshape(depth, 1, n))


def _norm_kernel(*refs, modulated, in_tokens, out_tokens, copy_out):
    refs = list(refs)
    x_ref, g_ref = refs[:2]
    scale_ref, shift_ref = refs[2:4] if modulated else (None, None)
    outs = refs[4 if modulated else 2:]
    o_ref = outs[0]
    copy_ref = outs[1] if copy_out else None
    stage_ref = outs[-1] if (in_tokens or out_tokens) else None
    tc = (x_ref.shape[1] // CHUNK) if in_tokens else x_ref.shape[2]
    lane_tiles = x_ref.shape[-1] // LANES
    plane_rows = lambda j: pl.ds(j, tc, stride=CHUNK)
    if in_tokens:
        for k in range(lane_tiles):
            stage_ref[k] = x_ref[0, :, k * LANES:(k + 1) * LANES]
    for j in range(CHUNK):
        if in_tokens:
            x = jnp.concatenate([stage_ref[k, plane_rows(j), :] for k in range(lane_tiles)], axis=-1)
        else:
            x = x_ref[0, j]
        y = x * lax.rsqrt(jnp.mean(x * x, axis=-1, keepdims=True) + RMS_EPS) * g_ref[...]
        if modulated:
            y = y * (1.0 + scale_ref[0]) + shift_ref[0]
        if out_tokens:
            for k in range(lane_tiles):
                stage_ref[k, plane_rows(j), :] = y[:, k * LANES:(k + 1) * LANES]
        else:
            o_ref[0, j] = y.astype(o_ref.dtype)
        if copy_out:
            copy_ref[0, j] = x
    if out_tokens:
        for k in range(lane_tiles):
            o_ref[0, :, k * LANES:(k + 1) * LANES] = stage_ref[k].astype(o_ref.dtype)


def _norm(x, g, scale, shift, *, in_tokens=False, out_tokens=False, copy_out=False, out_dtype=BF16):
    if in_tokens:
        bsz, seq, d = x.shape
        nchunk = seq // CHUNK
    else:
        bsz, _, nchunk, d = x.shape
    tc = min(16 if (in_tokens or out_tokens) else 32, nchunk)
    tok_spec = pl.BlockSpec((1, tc * CHUNK, d), lambda b, c: (b, c, 0))
    pos_spec = pl.BlockSpec((1, CHUNK, tc, d), lambda b, c: (b, 0, c, 0))
    modulated = scale is not None
    in_specs = [tok_spec if in_tokens else pos_spec, pl.BlockSpec((1, d), lambda b, c: (0, 0))]
    args = [x, g.reshape(1, d)]
    if modulated:
        mod_spec = pl.BlockSpec((1, 1, d), lambda b, c: (b, 0, 0))
        in_specs += [mod_spec, mod_spec]
        args += [scale, shift]
    pos_shape = (bsz, CHUNK, nchunk, d)
    out_shape = [jax.ShapeDtypeStruct((bsz, nchunk * CHUNK, d) if out_tokens else pos_shape, out_dtype)]
    out_specs = [tok_spec if out_tokens else pos_spec]
    if copy_out:
        out_shape.append(jax.ShapeDtypeStruct(pos_shape, F32))
        out_specs.append(pos_spec)
    staged = in_tokens or out_tokens
    scratch = [pltpu.VMEM((d // LANES, tc * CHUNK, LANES), F32)] if staged else []
    outs = pl.pallas_call(
        functools.partial(_norm_kernel, modulated=modulated, in_tokens=in_tokens, out_tokens=out_tokens,
                          copy_out=copy_out),
        out_shape=out_shape,
        grid=(bsz, nchunk // tc),
        in_specs=in_specs,
        out_specs=out_specs,
        scratch_shapes=scratch,
        compiler_params=_params("parallel", "parallel"),
    )(*args)
    return outs if copy_out else outs[0]


def _grid_stream(w_hbm, scratch, layer, steps, col_block0=0):
    return _WeightStream(w_hbm, *scratch, layer=layer, col_block0=col_block0, panel=pl.program_id(0),
                         n_panels=pl.num_programs(0), step=pl.program_id(1), steps=steps)


def _proj_kernel(a_ref, w_hbm, o_ref, *scratch, sigmoid, layer, steps, col_block0):
    stream = _grid_stream(w_hbm, scratch, layer, steps, col_block0)
    w = stream.begin()

    def block(cols):
        z = jnp.dot(a_ref[...], w[:, cols], preferred_element_type=F32)
        o_ref[:, cols] = (_sigmoid(z) if sigmoid else z).astype(o_ref.dtype)

    _for_col_blocks(o_ref.shape[1], block)
    stream.finish()


def _proj(a, w, layer, col0, n, *, sigmoid):
    rows, k = a.shape
    tm, tn = min(ROW_TILE, rows), min(COL_TILE, n)
    return pl.pallas_call(
        functools.partial(_proj_kernel, sigmoid=sigmoid, layer=layer, steps=rows // tm, col_block0=col0 // tn),
        out_shape=jax.ShapeDtypeStruct((rows, n), BF16),
        grid=(n // tn, rows // tm),
        in_specs=[pl.BlockSpec((tm, k), lambda j, i: (i, 0)), HBM_SPEC],
        out_specs=pl.BlockSpec((tm, tn), lambda j, i: (i, j)),
        scratch_shapes=_stream_scratch(k, tn, rows // tm),
        compiler_params=_params("arbitrary", "arbitrary"),
    )(a, w)


def _discretise(a_re, a_im, log_dt):
    lam_re = jnp.minimum(a_re, LAM_RE_MAX)
    lam_im = a_im
    dt = jnp.exp(log_dt)
    mag = jnp.exp(lam_re * dt)
    abar_re = mag * jnp.cos(lam_im * dt)
    abar_im = mag * jnp.sin(lam_im * dt)
    den = lam_re * lam_re + lam_im * lam_im
    x_re = abar_re - 1.0
    f_re = (x_re * lam_re + abar_im * lam_im) / den
    f_im = (abar_im * lam_re - x_re * lam_im) / den
    return abar_re, abar_im, f_re, f_im


def _split_bf16(x):
    hi = x.astype(BF16)
    return hi, (x - hi.astype(F32)).astype(BF16)


def _dot_split(a, b):
    a_hi, a_lo = a
    b_hi, b_lo = b
    dot = functools.partial(jnp.dot, preferred_element_type=F32)
    return dot(a_hi, b_hi) + (dot(a_hi, b_lo) + dot(a_lo, b_hi))


def _s5_prep_kernel(rowp_ref, bt_ref, ct_ref, kt_ref, p_ref, q_ref, dre_ref, dim_ref):
    ns, ch = SLICE_STATES, SLICE_CH
    ar, ai, f_re, f_im = _discretise(rowp_ref[0, 0, 0:1, :], rowp_ref[0, 0, 1:2, :], rowp_ref[0, 0, 2:3, :])
    bt_re = jnp.concatenate([bt_ref[0, 0, 0]] * SLICE_GROUPS, axis=0)
    bt_im = jnp.concatenate([bt_ref[0, 0, 1]] * SLICE_GROUPS, axis=0)
    same_group = (lax.broadcasted_iota(jnp.int32, (ch, ns), 0) // SSM_GROUP
                  == lax.broadcasted_iota(jnp.int32, (ch, ns), 1) // SSM_STATE)
    e_re = jnp.where(same_group, f_re * bt_re - f_im * bt_im, 0.0)
    e_im = jnp.where(same_group, f_re * bt_im + f_im * bt_re, 0.0)
    same_group_t = (lax.broadcasted_iota(jnp.int32, (ns, ch), 0) // SSM_STATE
                    == lax.broadcasted_iota(jnp.int32, (ns, ch), 1) // SSM_GROUP)
    c_re = jnp.where(same_group_t, ct_ref[0, 0, 0], 0.0)
    c_im = jnp.where(same_group_t, ct_ref[0, 0, 1], 0.0)
    c_re_split, c_im_split = _split_bf16(c_re), _split_bf16(c_im)
    for t in range(CHUNK):
        j = CHUNK - 1 - t
        e_re_split, e_im_split = _split_bf16(e_re), _split_bf16(e_im)
        p_ref[0, 0, j * ch:(j + 1) * ch, 0:ns] = e_re_split[0]
        p_ref[0, 0, j * ch:(j + 1) * ch, ns:2 * ns] = e_im_split[0]
        kt_ref[0, 0, t] = (_dot_split(e_re_split, c_re_split) - _dot_split(e_im_split, c_im_split)).astype(BF16)
        e_re, e_im = e_re * ar - e_im * ai, e_re * ai + e_im * ar

    d_re, d_im = ar, ai
    for _ in range(CHUNK.bit_length() - 1):
        d_re, d_im = d_re * d_re - d_im * d_im, 2.0 * d_re * d_im
    p_re, p_im = d_re, d_im
    for r in range(SUBLANES):
        dre_ref[0, 0, r:r + 1, :] = p_re
        dim_ref[0, 0, r:r + 1, :] = p_im
        p_re, p_im = p_re * d_re - p_im * d_im, p_re * d_im + p_im * d_re

    arc = jnp.transpose(jnp.broadcast_to(ar, (ch, ns)))
    aic = jnp.transpose(jnp.broadcast_to(ai, (ch, ns)))
    w_re, w_im = c_re, c_im
    for i in range(CHUNK):
        w_re, w_im = w_re * arc - w_im * aic, w_re * aic + w_im * arc
        q_ref[0, 0, 0:ns, i * ch:(i + 1) * ch] = w_re.astype(BF16)
        q_ref[0, 0, ns:2 * ns, i * ch:(i + 1) * ch] = (-w_im).astype(BF16)


def _s5_prep(a_re, a_im, log_dt, b_re, b_im, c_re, c_im):
    depth, groups, nst = a_re.shape
    nsl = groups // SLICE_GROUPS
    ns, ch = SLICE_STATES, SLICE_CH
    dt_b = jnp.broadcast_to(log_dt[:, :, None], a_re.shape)
    rows = jnp.stack([a_re, a_im, dt_b], axis=2).reshape(depth, nsl, SLICE_GROUPS, 3, nst)
    rows = jnp.swapaxes(rows, 2, 3).reshape(depth, nsl, 3, ns)
    rowp = jnp.pad(rows, ((0, 0), (0, 0), (0, SUBLANES - 3), (0, 0)))
    bt = jnp.stack([b_re, b_im], axis=1).reshape(depth, 2, nsl, ns, SSM_GROUP)
    bt = jnp.transpose(bt, (0, 2, 1, 4, 3))
    ct = jnp.stack([c_re, c_im], axis=1).reshape(depth, 2, nsl, SLICE_GROUPS, SSM_GROUP, nst)
    ct = jnp.transpose(ct, (0, 2, 1, 3, 5, 4)).reshape(depth, nsl, 2, ns, SSM_GROUP)
    ct = jnp.tile(ct, (1, 1, 1, 1, SLICE_GROUPS))
    out_shapes = (
        jax.ShapeDtypeStruct((depth, nsl, CHUNK, ch, ch), BF16),
        jax.ShapeDtypeStruct((depth, nsl, CHUNK * ch, 2 * ns), BF16),
        jax.ShapeDtypeStruct((depth, nsl, 2 * ns, CHUNK * ch), BF16),
        jax.ShapeDtypeStruct((depth, nsl, SUBLANES, ns), F32),
        jax.ShapeDtypeStruct((depth, nsl, SUBLANES, ns), F32),
    )
    return pl.pallas_call(
        _s5_prep_kernel,
        out_shape=out_shapes,
        grid=(depth, nsl),
        in_specs=[
            pl.BlockSpec((1, 1, SUBLANES, ns), lambda l, s: (l, s, 0, 0)),
            pl.BlockSpec((1, 1, 2, SSM_GROUP, ns), lambda l, s: (l, s, 0, 0, 0)),
            pl.BlockSpec((1, 1, 2, ns, ch), lambda l, s: (l, s, 0, 0, 0)),
        ],
        out_specs=(
            pl.BlockSpec((1, 1, CHUNK, ch, ch), lambda l, s: (l, s, 0, 0, 0)),
            pl.BlockSpec((1, 1, CHUNK * ch, 2 * ns), lambda l, s: (l, s, 0, 0)),
            pl.BlockSpec((1, 1, 2 * ns, CHUNK * ch), lambda l, s: (l, s, 0, 0)),
            pl.BlockSpec((1, 1, SUBLANES, ns), lambda l, s: (l, s, 0, 0)),
            pl.BlockSpec((1, 1, SUBLANES, ns), lambda l, s: (l, s, 0, 0)),
        ),
        compiler_params=_params("parallel", "parallel"),
    )(rowp, bt, ct)


def _chunk_scan_prev(x_re, x_im, dre_ref, dim_ref):
    nrows, ns = x_re.shape
    nblk = nrows // SUBLANES
    x_re = x_re.reshape(nblk, SUBLANES, ns)
    x_im = x_im.reshape(nblk, SUBLANES, ns)
    row = lax.broadcasted_iota(jnp.int32, (SUBLANES, ns), 0)
    sh = 1
    while sh < SUBLANES:
        d_re = jnp.where(row >= sh, dre_ref[0, 0, sh - 1:sh, :], 0.0)
        d_im = jnp.where(row >= sh, dim_ref[0, 0, sh - 1:sh, :], 0.0)
        s_re, s_im = pltpu.roll(x_re, sh, 1), pltpu.roll(x_im, sh, 1)
        x_re, x_im = x_re + d_re * s_re - d_im * s_im, x_im + d_re * s_im + d_im * s_re
        sh *= 2
    pw_re, pw_im = dre_ref[0, 0], dim_ref[0, 0]
    first_row = row == 0
    c_re = c_im = jnp.zeros((SUBLANES, ns), F32)
    prev_re, prev_im = [], []
    for blk in range(nblk):
        b_re = x_re[blk] + pw_re * c_re - pw_im * c_im
        b_im = x_im[blk] + pw_re * c_im + pw_im * c_re
        prev_re.append(jnp.where(first_row, c_re, pltpu.roll(b_re, 1, 0)))
        prev_im.append(jnp.where(first_row, c_im, pltpu.roll(b_im, 1, 0)))
        c_re = jnp.broadcast_to(b_re[SUBLANES - 1:SUBLANES], pw_re.shape)
        c_im = jnp.broadcast_to(b_im[SUBLANES - 1:SUBLANES], pw_re.shape)
    return jnp.concatenate(prev_re, axis=0), jnp.concatenate(prev_im, axis=0)


def _s5_kernel(u_ref, kt_ref, p_ref, q_ref, dre_ref, dim_ref, dskip_ref, o_ref, toep_ref, y_ref):
    ch, ns = SLICE_CH, SLICE_STATES
    per_blk = MXU_COLS // ch
    nblk = CHUNK // per_blk

    @pl.when(pl.program_id(1) == 0)
    def _():
        for i in range(CHUNK):
            last_j = (i // per_blk + 1) * per_blk
            for j in range(last_j):
                blk = kt_ref[0, 0, i - j] if j <= i else jnp.zeros((ch, ch), BF16)
                toep_ref[j * ch:(j + 1) * ch, i * ch:(i + 1) * ch] = blk

    u_all = jnp.concatenate([u_ref[0, j] for j in range(CHUNK)], axis=-1)
    x_loc = jnp.dot(u_all, p_ref[0, 0], preferred_element_type=F32)
    for t in range(nblk):
        cols = slice(t * MXU_COLS, (t + 1) * MXU_COLS)
        k_rows = (t + 1) * MXU_COLS
        y_ref[:, cols] = jnp.dot(u_all[:, :k_rows], toep_ref[:k_rows, cols], preferred_element_type=F32)
    x_re, x_im = _chunk_scan_prev(x_loc[:, :ns], x_loc[:, ns:], dre_ref, dim_ref)
    x_prev = jnp.concatenate([x_re, x_im], axis=-1).astype(BF16)
    for t in range(nblk):
        cols = slice(t * MXU_COLS, (t + 1) * MXU_COLS)
        y_blk = y_ref[:, cols] + jnp.dot(x_prev, q_ref[0, 0, :, cols], preferred_element_type=F32)
        for s in range(per_blk):
            i = t * per_blk + s
            y = y_blk[:, s * ch:(s + 1) * ch] + u_ref[0, i].astype(F32) * dskip_ref[0]
            o_ref[0, i] = jax.nn.gelu(y).astype(o_ref.dtype)


def _s5(u4, kt, p, q, dre, dim, d_skip, layer):
    bsz, _, nchunk, _ = u4.shape
    ch, ns = SLICE_CH, SLICE_STATES
    nsl = kt.shape[1]
    assert nchunk % SUBLANES == 0
    return pl.pallas_call(
        _s5_kernel,
        out_shape=jax.ShapeDtypeStruct((bsz, CHUNK, nchunk, nsl * ch), BF16),
        grid=(nsl, bsz),
        in_specs=[
            pl.BlockSpec((1, CHUNK, nchunk, ch), lambda s, b: (b, 0, 0, s)),
            pl.BlockSpec((1, 1, CHUNK, ch, ch), lambda s, b: (layer, s, 0, 0, 0)),
            pl.BlockSpec((1, 1, CHUNK * ch, 2 * ns), lambda s, b: (layer, s, 0, 0)),
            pl.BlockSpec((1, 1, 2 * ns, CHUNK * ch), lambda s, b: (layer, s, 0, 0)),
            pl.BlockSpec((1, 1, SUBLANES, ns), lambda s, b: (layer, s, 0, 0)),
            pl.BlockSpec((1, 1, SUBLANES, ns), lambda s, b: (layer, s, 0, 0)),
            pl.BlockSpec((1, 1, ch), lambda s, b: (layer, 0, s)),
        ],
        out_specs=pl.BlockSpec((1, CHUNK, nchunk, ch), lambda s, b: (b, 0, 0, s)),
        scratch_shapes=[pltpu.VMEM((CHUNK * ch, CHUNK * ch), BF16), pltpu.VMEM((nchunk, CHUNK * ch), F32)],
        compiler_params=_params("arbitrary", "arbitrary"),
    )(u4, kt, p, q, dre, dim, d_skip)


def _glu_kernel(a_ref, w_hbm, b_ref, y_ref, o_ref, *scratch, layer, steps):
    stream = _grid_stream(w_hbm, scratch, layer, steps)
    w = stream.begin()

    def block(cols):
        z = jnp.dot(a_ref[...], w[:, cols], preferred_element_type=F32) + b_ref[0, :, cols]
        o_ref[:, cols] = (y_ref[:, cols].astype(F32) * _sigmoid(z)).astype(o_ref.dtype)

    _for_col_blocks(o_ref.shape[1], block)
    stream.finish()


def _glu(y, w, b, layer):
    rows, k = y.shape
    n = w.shape[2]
    tm, tn = min(ROW_TILE, rows), min(COL_TILE, n)
    return pl.pallas_call(
        functools.partial(_glu_kernel, layer=layer, steps=rows // tm),
        out_shape=jax.ShapeDtypeStruct((rows, n), BF16),
        grid=(n // tn, rows // tm),
        in_specs=[
            pl.BlockSpec((tm, k), lambda j, i: (i, 0)),
            HBM_SPEC,
            pl.BlockSpec((1, 1, tn), lambda j, i: (layer, 0, j)),
            pl.BlockSpec((tm, tn), lambda j, i: (i, j)),
        ],
        out_specs=pl.BlockSpec((tm, tn), lambda j, i: (i, j)),
        scratch_shapes=_stream_scratch(k, tn, rows // tm),
        compiler_params=_params("arbitrary", "arbitrary"),
    )(y, w, b, y)


def _pool_kernel(v_ref, o_ref, *, blocks_per_group):
    nchunk = v_ref.shape[2]
    group = pl.program_id(1) // blocks_per_group
    first_row = lax.broadcasted_iota(jnp.int32, (nchunk, 1), 0) == 0

    def pooled(win):
        v = [v_ref[0, j].astype(F32) for j in range(CHUNK)]
        prefix = [v[0]]
        for j in range(1, CHUNK):
            prefix.append(prefix[-1] + v[j])
        for j in range(CHUNK):
            if j >= win:
                wsum = prefix[j] - prefix[j - win]
            elif j == win - 1:
                wsum = prefix[j]
            else:
                wsum = prefix[j] + _shift_rows(prefix[CHUNK - 1] - prefix[j - win + CHUNK], 1)
            count = jnp.where(first_row, float(min(j + 1, win)), float(win))
            o_ref[0, j] = (wsum / count - v[j]).astype(o_ref.dtype)

    for g, win in enumerate(POOL_WINDOWS):
        pl.when(group == g)(functools.partial(pooled, win))


def _pool(u4, col0, width):
    bsz, _, nchunk, _ = u4.shape
    tn = 128
    blocks_per_group = width // len(POOL_WINDOWS) // tn
    return pl.pallas_call(
        functools.partial(_pool_kernel, blocks_per_group=blocks_per_group),
        out_shape=jax.ShapeDtypeStruct((bsz, CHUNK, nchunk, width), BF16),
        grid=(bsz, width // tn),
        in_specs=[pl.BlockSpec((1, CHUNK, nchunk, tn), lambda b, t: (b, 0, 0, col0 // tn + t))],
        out_specs=pl.BlockSpec((1, CHUNK, nchunk, tn), lambda b, t: (b, 0, 0, t)),
        compiler_params=_params("parallel", "parallel"),
    )(u4)


def _pool_mix_kernel(a_ref, w_ref, b_ref, s_ref, o_ref):
    z = jnp.dot(a_ref[...], w_ref[0, 0].astype(BF16), preferred_element_type=F32)
    o_ref[...] = ((z + b_ref[0, 0]) * s_ref[0]).astype(o_ref.dtype)


def _pool_mix(pooled, w, b, scale, layer):
    rows, width = pooled.shape
    ngroups, gw = w.shape[1], w.shape[2]
    tm = min(2 * ROW_TILE, rows)
    return pl.pallas_call(
        _pool_mix_kernel,
        out_shape=jax.ShapeDtypeStruct((rows, width), BF16),
        grid=(ngroups, rows // tm),
        in_specs=[
            pl.BlockSpec((tm, gw), lambda g, i: (i, g)),
            pl.BlockSpec((1, 1, gw, gw), lambda g, i: (layer, g, 0, 0)),
            pl.BlockSpec((1, 1, 1, gw), lambda g, i: (layer, g, 0, 0)),
            pl.BlockSpec((1, 1, gw), lambda g, i: (layer, 0, g)),
        ],
        out_specs=pl.BlockSpec((tm, gw), lambda g, i: (i, g)),
        compiler_params=_params("parallel", "parallel"),
    )(pooled, w, b, scale)


def _merge_kernel(a1_ref, w1_hbm, a2_ref, w2_hbm, g1_ref, g2_ref, o_ref, *scratch, layer, steps):
    stream1 = _grid_stream(w1_hbm, scratch[:3], layer, steps)
    stream2 = _grid_stream(w2_hbm, scratch[3:], layer, steps)
    w1, w2 = stream1.begin(), stream2.begin()

    def block(cols):
        z1 = jnp.dot(a1_ref[...], w1[:, cols], preferred_element_type=F32)
        z2 = jnp.dot(a2_ref[...], w2[:, cols], preferred_element_type=F32)
        o_ref[:, cols] = (g1_ref[:, cols].astype(F32) * z1 + g2_ref[:, cols].astype(F32) * z2).astype(o_ref.dtype)

    _for_col_blocks(o_ref.shape[1], block)
    stream1.finish()
    stream2.finish()


def _merge(o_ssm, w_ssm_out, o_pool, w_pool_out, gates, layer):
    rows, k1 = o_ssm.shape
    k2 = o_pool.shape[1]
    n = w_ssm_out.shape[2]
    tm, tn = min(ROW_TILE, rows), min(COL_TILE, n)
    return pl.pallas_call(
        functools.partial(_merge_kernel, layer=layer, steps=rows // tm),
        out_shape=jax.ShapeDtypeStruct((rows, n), BF16),
        grid=(n // tn, rows // tm),
        in_specs=[
            pl.BlockSpec((tm, k1), lambda j, i: (i, 0)),
            HBM_SPEC,
            pl.BlockSpec((tm, k2), lambda j, i: (i, 0)),
            HBM_SPEC,
            pl.BlockSpec((tm, tn), lambda j, i: (i, j)),
            pl.BlockSpec((tm, tn), lambda j, i: (i, n // tn + j)),
        ],
        out_specs=pl.BlockSpec((tm, tn), lambda j, i: (i, j)),
        scratch_shapes=_stream_scratch(k1, tn, rows // tm) + _stream_scratch(k2, tn, rows // tm),
        compiler_params=_params("arbitrary", "arbitrary"),
    )(o_ssm, w_ssm_out, o_pool, w_pool_out, gates, gates)


def _residual_kernel(a_ref, w_hbm, h_ref, g_ref, o_ref, *scratch, layer, steps):
    stream = _grid_stream(w_hbm, scratch, layer, steps)
    w = stream.begin()

    def block(cols):
        z = jnp.dot(a_ref[...], w[:, cols], preferred_element_type=F32)
        o_ref[:, cols] = h_ref[:, cols] + g_ref[0, :, cols] * z

    _for_col_blocks(o_ref.shape[1], block)
    stream.finish()


def _residual(a, w, layer, h, gate, *, tm, tn):
    rows, k = a.shape
    n = w.shape[2]
    rows_per_seq = rows // gate.shape[0]
    tm, tn = min(tm, rows_per_seq), min(tn, n)
    return pl.pallas_call(
        functools.partial(_residual_kernel, layer=layer, steps=rows // tm),
        out_shape=jax.ShapeDtypeStruct((rows, n), F32),
        grid=(n // tn, rows // tm),
        in_specs=[
            pl.BlockSpec((tm, k), lambda j, i: (i, 0)),
            HBM_SPEC,
            pl.BlockSpec((tm, tn), lambda j, i: (i, j)),
            pl.BlockSpec((1, 1, tn), lambda j, i: (i * tm // rows_per_seq, 0, j)),
        ],
        out_specs=pl.BlockSpec((tm, tn), lambda j, i: (i, j)),
        scratch_shapes=_stream_scratch(k, tn, rows // tm),
        input_output_aliases={2: 0},
        compiler_params=_params("arbitrary", "arbitrary"),
    )(a, w, h, gate)


def _up_conv_kernel(a_ref, w_hbm, cwg_ref, cwv_ref, cbg_ref, cbv_ref, o_ref, carry_ref, *scratch, layer, steps):
    _, _, cm, d = a_ref.shape
    tn = o_ref.shape[-1]
    seq_start = pl.program_id(2) == 0
    common = dict(layer=layer, panel=pl.program_id(0), n_panels=pl.num_programs(0),
                  step=pl.program_id(1) * pl.num_programs(2) + pl.program_id(2), steps=steps)
    gate_stream = _WeightStream(w_hbm, *scratch[:3], col_block0=0, **common)
    val_stream = _WeightStream(w_hbm, *scratch[3:], col_block0=pl.num_programs(0), **common)
    wg, wv = gate_stream.begin(), val_stream.begin()

    @pl.when(seq_start)
    def _():
        carry_ref[...] = jnp.zeros(carry_ref.shape, carry_ref.dtype)

    a = a_ref[0].reshape(CHUNK * cm, d)

    products = {}

    def matmuls(cols):
        for name, w in (("gate", wg), ("val", wv)):
            products[name, cols.start] = jnp.dot(a, w[:, cols], preferred_element_type=F32)

    _for_col_blocks(tn, matmuls)

    def conv(name, cw_ref, cb_ref, slot, cols):
        sub = cols.stop - cols.start
        first_row = lax.broadcasted_iota(jnp.int32, (cm, sub), 0) == 0
        r = products[name, cols.start].reshape(CHUNK, cm, sub)
        prev = []
        for t in range(CONV_WIDTH - 1):
            plane = r[CHUNK - (CONV_WIDTH - 1) + t]
            prev.append(jnp.where(first_row, carry_ref[slot + t, 0:1, cols], pltpu.roll(plane, 1, 0)))
            carry_ref[slot + t, :, cols] = jnp.broadcast_to(plane[cm - 1:cm, :], (SUBLANES, sub))
        planes = prev + [r[j] for j in range(CHUNK)]
        outs = []
        for j in range(CHUNK):
            acc = cb_ref[0, :, cols]
            for t in range(CONV_WIDTH):
                acc = acc + cw_ref[0, t:t + 1, cols] * planes[j + t]
            outs.append(acc)
        return outs

    def block(cols):
        gate = conv("gate", cwg_ref, cbg_ref, 0, cols)
        val = conv("val", cwv_ref, cbv_ref, CONV_WIDTH - 1, cols)
        for j in range(CHUNK):
            o_ref[0, j, :, cols] = (_silu(gate[j]) * val[j]).astype(o_ref.dtype)

    _for_col_blocks(tn, block)
    gate_stream.finish()
    val_stream.finish()


def _up_conv(y4, w_up, conv_w, conv_b, layer):
    bsz, _, nchunk, d = y4.shape
    d_ff = w_up.shape[2] // 2
    cm, tn = min(ROW_TILE // CHUNK, nchunk), min(COL_TILE // 2, d_ff)
    nblk = d_ff // tn
    steps = bsz * (nchunk // cm)
    return pl.pallas_call(
        functools.partial(_up_conv_kernel, layer=layer, steps=steps),
        out_shape=jax.ShapeDtypeStruct((bsz, CHUNK, nchunk, d_ff), BF16),
        grid=(nblk, bsz, nchunk // cm),
        in_specs=[
            pl.BlockSpec((1, CHUNK, cm, d), lambda n, b, c: (b, 0, c, 0)),
            HBM_SPEC,
            pl.BlockSpec((1, CONV_WIDTH, tn), lambda n, b, c: (layer, 0, n)),
            pl.BlockSpec((1, CONV_WIDTH, tn), lambda n, b, c: (layer, 0, nblk + n)),
            pl.BlockSpec((1, 1, tn), lambda n, b, c: (layer, 0, n)),
            pl.BlockSpec((1, 1, tn), lambda n, b, c: (layer, 0, nblk + n)),
        ],
        out_specs=pl.BlockSpec((1, CHUNK, cm, tn), lambda n, b, c: (b, 0, c, n)),
        scratch_shapes=([pltpu.VMEM((2 * (CONV_WIDTH - 1), SUBLANES, tn), F32)]
                        + _stream_scratch(d, tn, steps) + _stream_scratch(d, tn, steps)),
        compiler_params=_params("arbitrary", "arbitrary", "arbitrary"),
    )(y4, w_up, conv_w, conv_w, conv_b, conv_b)


def kernel(x, c, w_cond, b_cond, ada_table, norm1_g, norm2_g, w_in, ssm_a_re, ssm_a_im, ssm_log_dt, ssm_b_re, ssm_b_im, ssm_c_re, ssm_c_im, ssm_d, w_glu, b_glu, w_pool, b_pool, pool_scale, w_ssm_out, w_pool_out, w_o, w_up, conv_w, conv_b, w_down, final_g):
    bsz, seq, d = x.shape
    depth = w_in.shape[0]
    nchunk = seq // CHUNK
    rows = bsz * seq
    ssm_width = w_glu.shape[2]
    pool_width = pool_scale.shape[1]
    assert seq % CHUNK == 0 and ssm_width % SLICE_CH == 0

    c_pad = jnp.pad(c, ((0, SUBLANES - bsz), (0, 0)))
    mods = _cond(c_pad, w_cond, b_cond, ada_table)[:, :bsz].reshape(depth, bsz, N_MOD, 1, d)
    kt, p, q, dre, dim = _s5_prep(ssm_a_re, ssm_a_im, ssm_log_dt, ssm_b_re, ssm_b_im, ssm_c_re, ssm_c_im)
    d_skip = ssm_d.reshape(depth, 1, ssm_width)
    b_glu = b_glu.reshape(depth, 1, ssm_width)
    b_pool = b_pool[:, :, None, :]
    pool_scale = pool_scale.reshape(depth, 1, pool_width)
    conv_b = conv_b.reshape(depth, 1, -1)

    h = None
    for l in range(depth):
        shift1, scale1, gate1, shift2, scale2, gate2 = (mods[l, :, i] for i in range(N_MOD))
        if l == 0:
            y, h = _norm(x, norm1_g[l], scale1, shift1, in_tokens=True, copy_out=True)
            h = h.reshape(rows, d)
        else:
            y = _norm(h.reshape(bsz, CHUNK, nchunk, d), norm1_g[l], scale1, shift1)
        y = y.reshape(rows, d)
        u = _proj(y, w_in, l, 0, ssm_width + pool_width, sigmoid=False)
        gates = _proj(y, w_in, l, ssm_width + pool_width, 2 * d, sigmoid=True)
        u4 = u.reshape(bsz, CHUNK, nchunk, -1)
        y_ssm = _s5(u4, kt, p, q, dre, dim, d_skip, l).reshape(rows, ssm_width)
        o_ssm = _glu(y_ssm, w_glu, b_glu, l)
        pooled = _pool(u4, ssm_width, pool_width).reshape(rows, pool_width)
        o_pool = _pool_mix(pooled, w_pool, b_pool, pool_scale, l)
        merged = _merge(o_ssm, w_ssm_out, o_pool, w_pool_out, gates, l)
        h = _residual(merged, w_o, l, h, gate1, tm=ROW_TILE, tn=COL_TILE)

        y = _norm(h.reshape(bsz, CHUNK, nchunk, d), norm2_g[l], scale2, shift2)
        act = _up_conv(y, w_up, conv_w, conv_b, l).reshape(rows, -1)
        h = _residual(act, w_down, l, h, gate2, tm=ROW_TILE // 2, tn=COL_TILE // 2)

    return _norm(h.reshape(bsz, CHUNK, nchunk, d), final_g, None, None, out_tokens=True, out_dtype=x.dtype)
```

```python
import functools

import jax
import jax.numpy as jnp
from jax import lax
from jax.experimental import pallas as pl
from jax.experimental.pallas import tpu as pltpu

F32 = jnp.float32
BF16 = jnp.bfloat16

CHUNK = 16
SSM_GROUP = 16
SSM_STATE = 64
SLICE_CH = 128
SLICE_GROUPS = SLICE_CH // SSM_GROUP
SLICE_STATES = SLICE_GROUPS * SSM_STATE
POOL_WINDOWS = (2, 4, 8, 16)
CONV_WIDTH = 3
N_MOD = 6
RMS_EPS = 1e-6
LAM_RE_MAX = -1e-4
SUBLANES = 8
LANES = 128
MXU_COLS = 256
V7X_VMEM_LIMIT = 56 * 1024 * 1024
ROW_TILE = 1024
COL_TILE = 1024
MAX_WEIGHT_CHUNKS = 8
ROW_PARTS = 2


def _params(*sem):
    return pltpu.CompilerParams(dimension_semantics=sem, vmem_limit_bytes=V7X_VMEM_LIMIT)


def _shift_rows(x, sh):
    rolled = pltpu.roll(x, sh, 0)
    idx = lax.broadcasted_iota(jnp.int32, x.shape, 0)
    return jnp.where(idx < sh, jnp.zeros_like(x), rolled)


class _WeightStream:
    def __init__(self, w_hbm, wbf_ref, stage_ref, sem, *, layer, col_block0, panel, n_panels, step, steps):
        self.w_hbm, self.wbf, self.stage, self.sem = w_hbm, wbf_ref, stage_ref, sem
        self.layer, self.col_block0 = layer, col_block0
        self.panel, self.step = panel, step
        _, self.k, self.tn = wbf_ref.shape
        self.kc = stage_ref.shape[1]
        self.n_chunks = self.k // self.kc
        self.lag = 1 if self.n_chunks + 1 <= steps else 0
        assert self.n_chunks + self.lag <= steps
        has_next = panel + 1 < n_panels
        self.starting = jnp.logical_and(has_next, step < self.n_chunks)
        self.finishing = jnp.logical_and(has_next, jnp.logical_and(step >= self.lag, step < self.n_chunks + self.lag))

    def _rows(self, chunk):
        start = chunk * self.kc
        return pl.ds(start if isinstance(start, int) else pl.multiple_of(start, self.kc), self.kc)

    def _copy(self, panel, chunk, slot):
        col = (self.col_block0 + panel) * self.tn
        cols = pl.ds(col if isinstance(col, int) else pl.multiple_of(col, self.tn), self.tn)
        return pltpu.make_async_copy(self.w_hbm.at[self.layer, self._rows(chunk), cols],
                                     self.stage.at[slot], self.sem.at[slot])

    def _cast(self, wbf_slot, chunk, slot):
        self.wbf[wbf_slot, self._rows(chunk), :] = self.stage[slot].astype(BF16)

    def begin(self):
        @pl.when(jnp.logical_and(self.panel == 0, self.step == 0))
        def _():
            self._copy(0, 0, 0).start()
            for c in range(self.n_chunks):
                if c + 1 < self.n_chunks:
                    self._copy(0, c + 1, (c + 1) % 2).start()
                self._copy(0, c, c % 2).wait()
                self._cast(0, c, c % 2)

        @pl.when(self.starting)
        def _():
            self._copy(self.panel + 1, self.step, self.step % 2).start()

        return self.wbf.at[self.panel % 2]

    def finish(self):
        @pl.when(self.finishing)
        def _():
            chunk = self.step - self.lag
            self._copy(self.panel + 1, chunk, chunk % 2).wait()
            self._cast((self.panel + 1) % 2, chunk, chunk % 2)


def _stream_scratch(k, tn, steps):
    n_chunks = 1
    while 2 * n_chunks <= min(steps - 1, MAX_WEIGHT_CHUNKS):
        n_chunks *= 2
    return [pltpu.VMEM((2, k, tn), BF16), pltpu.VMEM((2, k // n_chunks, tn), F32), pltpu.SemaphoreType.DMA((2,))]


HBM_SPEC = pl.BlockSpec(memory_space=pl.ANY)


def _sigmoid(x):
    return 0.5 * jnp.tanh(0.5 * x) + 0.5


def _silu(x):
    h = 0.5 * x
    return h * jnp.tanh(h) + h


def _for_col_blocks(width, body):
    sub = min(MXU_COLS, width)
    for j in range(width // sub):
        body(slice(j * sub, (j + 1) * sub))


def _for_blocks(rows, width, body):
    parts = ROW_PARTS if rows >= ROW_TILE else 1
    part = rows // parts
    for r in range(parts):
        _for_col_blocks(width, functools.partial(body, slice(r * part, (r + 1) * part)))


def _cond_kernel(c_ref, w_ref, b_ref, ada_ref, o_ref):
    a = jax.nn.silu(c_ref[...]).astype(BF16)
    z = jnp.dot(a, w_ref[...].astype(BF16), preferred_element_type=F32) + b_ref[...]
    o_ref[...] = z[None] + ada_ref[...]


def _cond(c_pad, w_cond, b_cond, ada_table):
    rows, d = c_pad.shape
    n = w_cond.shape[1]
    depth = ada_table.shape[0]
    tn = min(COL_TILE // 2, n)
    return pl.pallas_call(
        _cond_kernel,
        out_shape=jax.ShapeDtypeStruct((depth, rows, n), F32),
        grid=(n // tn,),
        in_specs=[
            pl.BlockSpec((rows, d), lambda i: (0, 0)),
            pl.BlockSpec((d, tn), lambda i: (0, i)),
            pl.BlockSpec((1, tn), lambda i: (0, i)),
            pl.BlockSpec((depth, 1, tn), lambda i: (0, 0, i)),
        ],
        out_specs=pl.BlockSpec((depth, rows, tn), lambda i: (0, 0, i)),
        compiler_params=_params("arbitrary"),
    )(c_pad, w_cond, b_cond.reshape(1, n), ada_table.reshape(depth, 1, n))


def _norm_kernel(*refs, modulated, in_tokens, out_tokens, copy_out):
    refs = list(refs)
    x_ref, g_ref = refs[:2]
    scale_ref, shift_ref = refs[2:4] if modulated else (None, None)
    outs = refs[4 if modulated else 2:]
    o_ref = outs[0]
    copy_ref = outs[1] if copy_out else None
    stage_ref = outs[-1] if (in_tokens or out_tokens) else None
    tc = (x_ref.shape[1] // CHUNK) if in_tokens else x_ref.shape[2]
    lane_tiles = x_ref.shape[-1] // LANES
    plane_rows = lambda j: pl.ds(j, tc, stride=CHUNK)
    if in_tokens:
        for k in range(lane_tiles):
            stage_ref[k] = x_ref[0, :, k * LANES:(k + 1) * LANES]
    for j in range(CHUNK):
        if in_tokens:
            x = jnp.concatenate([stage_ref[k, plane_rows(j), :] for k in range(lane_tiles)], axis=-1)
        else:
            x = x_ref[0, j]
        y = x * lax.rsqrt(jnp.mean(x * x, axis=-1, keepdims=True) + RMS_EPS) * g_ref[...]
        if modulated:
            y = y * (1.0 + scale_ref[0]) + shift_ref[0]
        if out_tokens:
            for k in range(lane_tiles):
                stage_ref[k, plane_rows(j), :] = y[:, k * LANES:(k + 1) * LANES]
        else:
            o_ref[0, j] = y.astype(o_ref.dtype)
        if copy_out:
            copy_ref[0, j] = x
    if out_tokens:
        for k in range(lane_tiles):
            o_ref[0, :, k * LANES:(k + 1) * LANES] = stage_ref[k].astype(o_ref.dtype)


def _norm(x, g, scale, shift, *, in_tokens=False, out_tokens=False, copy_out=False, out_dtype=BF16):
    if in_tokens:
        bsz, seq, d = x.shape
        nchunk = seq // CHUNK
    else:
        bsz, _, nchunk, d = x.shape
    tc = min(16 if (in_tokens or out_tokens) else 64, nchunk)
    tok_spec = pl.BlockSpec((1, tc * CHUNK, d), lambda b, c: (b, c, 0))
    pos_spec = pl.BlockSpec((1, CHUNK, tc, d), lambda b, c: (b, 0, c, 0))
    modulated = scale is not None
    in_specs = [tok_spec if in_tokens else pos_spec, pl.BlockSpec((1, d), lambda b, c: (0, 0))]
    args = [x, g.reshape(1, d)]
    if modulated:
        mod_spec = pl.BlockSpec((1, 1, d), lambda b, c: (b, 0, 0))
        in_specs += [mod_spec, mod_spec]
        args += [scale, shift]
    pos_shape = (bsz, CHUNK, nchunk, d)
    out_shape = [jax.ShapeDtypeStruct((bsz, nchunk * CHUNK, d) if out_tokens else pos_shape, out_dtype)]
    out_specs = [tok_spec if out_tokens else pos_spec]
    if copy_out:
        out_shape.append(jax.ShapeDtypeStruct(pos_shape, F32))
        out_specs.append(pos_spec)
    staged = in_tokens or out_tokens
    scratch = [pltpu.VMEM((d // LANES, tc * CHUNK, LANES), F32)] if staged else []
    outs = pl.pallas_call(
        functools.partial(_norm_kernel, modulated=modulated, in_tokens=in_tokens, out_tokens=out_tokens,
                          copy_out=copy_out),
        out_shape=out_shape,
        grid=(bsz, nchunk // tc),
        in_specs=in_specs,
        out_specs=out_specs,
        scratch_shapes=scratch,
        compiler_params=_params("parallel", "parallel"),
    )(*args)
    return outs if copy_out else outs[0]


def _grid_stream(w_hbm, scratch, layer, steps, col_block0=0):
    return _WeightStream(w_hbm, *scratch, layer=layer, col_block0=col_block0, panel=pl.program_id(0),
                         n_panels=pl.num_programs(0), step=pl.program_id(1), steps=steps)


def _proj_kernel(a_ref, w_hbm, o_ref, *scratch, sigmoid, layer, steps, col_block0):
    stream = _grid_stream(w_hbm, scratch, layer, steps, col_block0)
    w = stream.begin()

    def block(rows, cols):
        z = jnp.dot(a_ref[rows, :], w[:, cols], preferred_element_type=F32)
        o_ref[rows, cols] = (_sigmoid(z) if sigmoid else z).astype(o_ref.dtype)

    _for_blocks(*o_ref.shape, block)
    stream.finish()


def _proj(a, w, layer, col0, n, *, sigmoid):
    rows, k = a.shape
    tm, tn = min(ROW_TILE, rows), min(COL_TILE, n)
    return pl.pallas_call(
        functools.partial(_proj_kernel, sigmoid=sigmoid, layer=layer, steps=rows // tm, col_block0=col0 // tn),
        out_shape=jax.ShapeDtypeStruct((rows, n), BF16),
        grid=(n // tn, rows // tm),
        in_specs=[pl.BlockSpec((tm, k), lambda j, i: (i, 0)), HBM_SPEC],
        out_specs=pl.BlockSpec((tm, tn), lambda j, i: (i, j)),
        scratch_shapes=_stream_scratch(k, tn, rows // tm),
        compiler_params=_params("arbitrary", "arbitrary"),
    )(a, w)


def _discretise(a_re, a_im, log_dt):
    lam_re = jnp.minimum(a_re, LAM_RE_MAX)
    lam_im = a_im
    dt = jnp.exp(log_dt)
    mag = jnp.exp(lam_re * dt)
    abar_re = mag * jnp.cos(lam_im * dt)
    abar_im = mag * jnp.sin(lam_im * dt)
    den = lam_re * lam_re + lam_im * lam_im
    x_re = abar_re - 1.0
    f_re = (x_re * lam_re + abar_im * lam_im) / den
    f_im = (abar_im * lam_re - x_re * lam_im) / den
    return abar_re, abar_im, f_re, f_im


def _split_bf16(x):
    hi = x.astype(BF16)
    return hi, (x - hi.astype(F32)).astype(BF16)


def _dot_split(a, b):
    a_hi, a_lo = a
    b_hi, b_lo = b
    dot = functools.partial(jnp.dot, preferred_element_type=F32)
    return dot(a_hi, b_hi) + (dot(a_hi, b_lo) + dot(a_lo, b_hi))


def _s5_prep_kernel(rowp_ref, bt_ref, ct_ref, kt_ref, p_ref, q_ref, dre_ref, dim_ref):
    ns, ch = SLICE_STATES, SLICE_CH
    ar, ai, f_re, f_im = _discretise(rowp_ref[0, 0, 0:1, :], rowp_ref[0, 0, 1:2, :], rowp_ref[0, 0, 2:3, :])
    bt_re = jnp.concatenate([bt_ref[0, 0, 0]] * SLICE_GROUPS, axis=0)
    bt_im = jnp.concatenate([bt_ref[0, 0, 1]] * SLICE_GROUPS, axis=0)
    same_group = (lax.broadcasted_iota(jnp.int32, (ch, ns), 0) // SSM_GROUP
                  == lax.broadcasted_iota(jnp.int32, (ch, ns), 1) // SSM_STATE)
    e_re = jnp.where(same_group, f_re * bt_re - f_im * bt_im, 0.0)
    e_im = jnp.where(same_group, f_re * bt_im + f_im * bt_re, 0.0)
    same_group_t = (lax.broadcasted_iota(jnp.int32, (ns, ch), 0) // SSM_STATE
                    == lax.broadcasted_iota(jnp.int32, (ns, ch), 1) // SSM_GROUP)
    c_re = jnp.where(same_group_t, ct_ref[0, 0, 0], 0.0)
    c_im = jnp.where(same_group_t, ct_ref[0, 0, 1], 0.0)
    c_re_split, c_im_split = _split_bf16(c_re), _split_bf16(c_im)
    for t in range(CHUNK):
        j = CHUNK - 1 - t
        e_re_split, e_im_split = _split_bf16(e_re), _split_bf16(e_im)
        p_ref[0, 0, j * ch:(j + 1) * ch, 0:ns] = e_re_split[0]
        p_ref[0, 0, j * ch:(j + 1) * ch, ns:2 * ns] = e_im_split[0]
        kt_ref[0, 0, t] = (_dot_split(e_re_split, c_re_split) - _dot_split(e_im_split, c_im_split)).astype(BF16)
        e_re, e_im = e_re * ar - e_im * ai, e_re * ai + e_im * ar

    d_re, d_im = ar, ai
    for _ in range(CHUNK.bit_length() - 1):
        d_re, d_im = d_re * d_re - d_im * d_im, 2.0 * d_re * d_im
    p_re, p_im = d_re, d_im
    for r in range(SUBLANES):
        dre_ref[0, 0, r:r + 1, :] = p_re
        dim_ref[0, 0, r:r + 1, :] = p_im
        p_re, p_im = p_re * d_re - p_im * d_im, p_re * d_im + p_im * d_re

    arc = jnp.transpose(jnp.broadcast_to(ar, (ch, ns)))
    aic = jnp.transpose(jnp.broadcast_to(ai, (ch, ns)))
    w_re, w_im = c_re, c_im
    for i in range(CHUNK):
        w_re, w_im = w_re * arc - w_im * aic, w_re * aic + w_im * arc
        q_ref[0, 0, 0:ns, i * ch:(i + 1) * ch] = w_re.astype(BF16)
        q_ref[0, 0, ns:2 * ns, i * ch:(i + 1) * ch] = (-w_im).astype(BF16)


def _s5_prep(a_re, a_im, log_dt, b_re, b_im, c_re, c_im):
    depth, groups, nst = a_re.shape
    nsl = groups // SLICE_GROUPS
    ns, ch = SLICE_STATES, SLICE_CH
    dt_b = jnp.broadcast_to(log_dt[:, :, None], a_re.shape)
    rows = jnp.stack([a_re, a_im, dt_b], axis=2).reshape(depth, nsl, SLICE_GROUPS, 3, nst)
    rows = jnp.swapaxes(rows, 2, 3).reshape(depth, nsl, 3, ns)
    rowp = jnp.pad(rows, ((0, 0), (0, 0), (0, SUBLANES - 3), (0, 0)))
    bt = jnp.stack([b_re, b_im], axis=1).reshape(depth, 2, nsl, ns, SSM_GROUP)
    bt = jnp.transpose(bt, (0, 2, 1, 4, 3))
    ct = jnp.stack([c_re, c_im], axis=1).reshape(depth, 2, nsl, SLICE_GROUPS, SSM_GROUP, nst)
    ct = jnp.transpose(ct, (0, 2, 1, 3, 5, 4)).reshape(depth, nsl, 2, ns, SSM_GROUP)
    ct = jnp.tile(ct, (1, 1, 1, 1, SLICE_GROUPS))
    out_shapes = (
        jax.ShapeDtypeStruct((depth, nsl, CHUNK, ch, ch), BF16),
        jax.ShapeDtypeStruct((depth, nsl, CHUNK * ch, 2 * ns), BF16),
        jax.ShapeDtypeStruct((depth, nsl, 2 * ns, CHUNK * ch), BF16),
        jax.ShapeDtypeStruct((depth, nsl, SUBLANES, ns), F32),
        jax.ShapeDtypeStruct((depth, nsl, SUBLANES, ns), F32),
    )
    return pl.pallas_call(
        _s5_prep_kernel,
        out_shape=out_shapes,
        grid=(depth, nsl),
        in_specs=[
            pl.BlockSpec((1, 1, SUBLANES, ns), lambda l, s: (l, s, 0, 0)),
            pl.BlockSpec((1, 1, 2, SSM_GROUP, ns), lambda l, s: (l, s, 0, 0, 0)),
            pl.BlockSpec((1, 1, 2, ns, ch), lambda l, s: (l, s, 0, 0, 0)),
        ],
        out_specs=(
            pl.BlockSpec((1, 1, CHUNK, ch, ch), lambda l, s: (l, s, 0, 0, 0)),
            pl.BlockSpec((1, 1, CHUNK * ch, 2 * ns), lambda l, s: (l, s, 0, 0)),
            pl.BlockSpec((1, 1, 2 * ns, CHUNK * ch), lambda l, s: (l, s, 0, 0)),
            pl.BlockSpec((1, 1, SUBLANES, ns), lambda l, s: (l, s, 0, 0)),
            pl.BlockSpec((1, 1, SUBLANES, ns), lambda l, s: (l, s, 0, 0)),
        ),
        compiler_params=_params("parallel", "parallel"),
    )(rowp, bt, ct)


def _chunk_scan_prev(x_re, x_im, dre_ref, dim_ref):
    nrows, ns = x_re.shape
    nblk = nrows // SUBLANES
    x_re = x_re.reshape(nblk, SUBLANES, ns)
    x_im = x_im.reshape(nblk, SUBLANES, ns)
    row = lax.broadcasted_iota(jnp.int32, (SUBLANES, ns), 0)
    sh = 1
    while sh < SUBLANES:
        d_re = jnp.where(row >= sh, dre_ref[0, 0, sh - 1:sh, :], 0.0)
        d_im = jnp.where(row >= sh, dim_ref[0, 0, sh - 1:sh, :], 0.0)
        s_re, s_im = pltpu.roll(x_re, sh, 1), pltpu.roll(x_im, sh, 1)
        x_re, x_im = x_re + d_re * s_re - d_im * s_im, x_im + d_re * s_im + d_im * s_re
        sh *= 2
    pw_re, pw_im = dre_ref[0, 0], dim_ref[0, 0]
    first_row = row == 0
    c_re = c_im = jnp.zeros((SUBLANES, ns), F32)
    prev_re, prev_im = [], []
    for blk in range(nblk):
        b_re = x_re[blk] + pw_re * c_re - pw_im * c_im
        b_im = x_im[blk] + pw_re * c_im + pw_im * c_re
        prev_re.append(jnp.where(first_row, c_re, pltpu.roll(b_re, 1, 0)))
        prev_im.append(jnp.where(first_row, c_im, pltpu.roll(b_im, 1, 0)))
        c_re = jnp.broadcast_to(b_re[SUBLANES - 1:SUBLANES], pw_re.shape)
        c_im = jnp.broadcast_to(b_im[SUBLANES - 1:SUBLANES], pw_re.shape)
    return jnp.concatenate(prev_re, axis=0), jnp.concatenate(prev_im, axis=0)


def _s5_kernel(u_ref, kt_ref, p_ref, q_ref, dre_ref, dim_ref, dskip_ref, o_ref, toep_ref, y_ref):
    ch, ns = SLICE_CH, SLICE_STATES
    per_blk = MXU_COLS // ch
    nblk = CHUNK // per_blk

    @pl.when(pl.program_id(1) == 0)
    def _():
        for i in range(CHUNK):
            last_j = (i // per_blk + 1) * per_blk
            for j in range(last_j):
                blk = kt_ref[0, 0, i - j] if j <= i else jnp.zeros((ch, ch), BF16)
                toep_ref[j * ch:(j + 1) * ch, i * ch:(i + 1) * ch] = blk

    u_all = jnp.concatenate([u_ref[0, j] for j in range(CHUNK)], axis=-1)
    x_loc = jnp.dot(u_all, p_ref[0, 0], preferred_element_type=F32)
    for t in range(nblk):
        cols = slice(t * MXU_COLS, (t + 1) * MXU_COLS)
        k_rows = (t + 1) * MXU_COLS
        y_ref[:, cols] = jnp.dot(u_all[:, :k_rows], toep_ref[:k_rows, cols], preferred_element_type=F32)
    x_re, x_im = _chunk_scan_prev(x_loc[:, :ns], x_loc[:, ns:], dre_ref, dim_ref)
    x_prev = jnp.concatenate([x_re, x_im], axis=-1).astype(BF16)
    for t in range(nblk):
        cols = slice(t * MXU_COLS, (t + 1) * MXU_COLS)
        y_blk = y_ref[:, cols] + jnp.dot(x_prev, q_ref[0, 0, :, cols], preferred_element_type=F32)
        for s in range(per_blk):
            i = t * per_blk + s
            y = y_blk[:, s * ch:(s + 1) * ch] + u_ref[0, i].astype(F32) * dskip_ref[0]
            o_ref[0, i] = jax.nn.gelu(y).astype(o_ref.dtype)


def _s5(u4, kt, p, q, dre, dim, d_skip, layer):
    bsz, _, nchunk, _ = u4.shape
    ch, ns = SLICE_CH, SLICE_STATES
    nsl = kt.shape[1]
    assert nchunk % SUBLANES == 0
    return pl.pallas_call(
        _s5_kernel,
        out_shape=jax.ShapeDtypeStruct((bsz, CHUNK, nchunk, nsl * ch), BF16),
        grid=(nsl, bsz),
        in_specs=[
            pl.BlockSpec((1, CHUNK, nchunk, ch), lambda s, b: (b, 0, 0, s)),
            pl.BlockSpec((1, 1, CHUNK, ch, ch), lambda s, b: (layer, s, 0, 0, 0)),
            pl.BlockSpec((1, 1, CHUNK * ch, 2 * ns), lambda s, b: (layer, s, 0, 0)),
            pl.BlockSpec((1, 1, 2 * ns, CHUNK * ch), lambda s, b: (layer, s, 0, 0)),
            pl.BlockSpec((1, 1, SUBLANES, ns), lambda s, b: (layer, s, 0, 0)),
            pl.BlockSpec((1, 1, SUBLANES, ns), lambda s, b: (layer, s, 0, 0)),
            pl.BlockSpec((1, 1, ch), lambda s, b: (layer, 0, s)),
        ],
        out_specs=pl.BlockSpec((1, CHUNK, nchunk, ch), lambda s, b: (b, 0, 0, s)),
        scratch_shapes=[pltpu.VMEM((CHUNK * ch, CHUNK * ch), BF16), pltpu.VMEM((nchunk, CHUNK * ch), F32)],
        compiler_params=_params("arbitrary", "arbitrary"),
    )(u4, kt, p, q, dre, dim, d_skip)


def _glu_kernel(a_ref, w_hbm, b_ref, y_ref, o_ref, *scratch, layer, steps):
    stream = _grid_stream(w_hbm, scratch, layer, steps)
    w = stream.begin()

    def block(rows, cols):
        z = jnp.dot(a_ref[rows, :], w[:, cols], preferred_element_type=F32) + b_ref[0, :, cols]
        o_ref[rows, cols] = (y_ref[rows, cols].astype(F32) * _sigmoid(z)).astype(o_ref.dtype)

    _for_blocks(*o_ref.shape, block)
    stream.finish()


def _glu(y, w, b, layer):
    rows, k = y.shape
    n = w.shape[2]
    tm, tn = min(ROW_TILE, rows), min(COL_TILE, n)
    return pl.pallas_call(
        functools.partial(_glu_kernel, layer=layer, steps=rows // tm),
        out_shape=jax.ShapeDtypeStruct((rows, n), BF16),
        grid=(n // tn, rows // tm),
        in_specs=[
            pl.BlockSpec((tm, k), lambda j, i: (i, 0)),
            HBM_SPEC,
            pl.BlockSpec((1, 1, tn), lambda j, i: (layer, 0, j)),
            pl.BlockSpec((tm, tn), lambda j, i: (i, j)),
        ],
        out_specs=pl.BlockSpec((tm, tn), lambda j, i: (i, j)),
        scratch_shapes=_stream_scratch(k, tn, rows // tm),
        compiler_params=_params("arbitrary", "arbitrary"),
    )(y, w, b, y)


def _pool_kernel(v_ref, o_ref, *, blocks_per_group):
    nchunk = v_ref.shape[2]
    group = pl.program_id(1) // blocks_per_group
    first_row = lax.broadcasted_iota(jnp.int32, (nchunk, 1), 0) == 0

    def pooled(win):
        v = [v_ref[0, j].astype(F32) for j in range(CHUNK)]
        prefix = [v[0]]
        for j in range(1, CHUNK):
            prefix.append(prefix[-1] + v[j])
        for j in range(CHUNK):
            if j >= win:
                wsum = prefix[j] - prefix[j - win]
            elif j == win - 1:
                wsum = prefix[j]
            else:
                wsum = prefix[j] + _shift_rows(prefix[CHUNK - 1] - prefix[j - win + CHUNK], 1)
            count = jnp.where(first_row, float(min(j + 1, win)), float(win))
            o_ref[0, j] = (wsum / count - v[j]).astype(o_ref.dtype)

    for g, win in enumerate(POOL_WINDOWS):
        pl.when(group == g)(functools.partial(pooled, win))


def _pool(u4, col0, width):
    bsz, _, nchunk, _ = u4.shape
    tn = 128
    blocks_per_group = width // len(POOL_WINDOWS) // tn
    return pl.pallas_call(
        functools.partial(_pool_kernel, blocks_per_group=blocks_per_group),
        out_shape=jax.ShapeDtypeStruct((bsz, CHUNK, nchunk, width), BF16),
        grid=(bsz, width // tn),
        in_specs=[pl.BlockSpec((1, CHUNK, nchunk, tn), lambda b, t: (b, 0, 0, col0 // tn + t))],
        out_specs=pl.BlockSpec((1, CHUNK, nchunk, tn), lambda b, t: (b, 0, 0, t)),
        compiler_params=_params("parallel", "parallel"),
    )(u4)


def _pool_mix_kernel(a_ref, w_ref, b_ref, s_ref, o_ref):
    z = jnp.dot(a_ref[...], w_ref[0, 0].astype(BF16), preferred_element_type=F32)
    o_ref[...] = ((z + b_ref[0, 0]) * s_ref[0]).astype(o_ref.dtype)


def _pool_mix(pooled, w, b, scale, layer):
    rows, width = pooled.shape
    ngroups, gw = w.shape[1], w.shape[2]
    tm = min(2 * ROW_TILE, rows)
    return pl.pallas_call(
        _pool_mix_kernel,
        out_shape=jax.ShapeDtypeStruct((rows, width), BF16),
        grid=(ngroups, rows // tm),
        in_specs=[
            pl.BlockSpec((tm, gw), lambda g, i: (i, g)),
            pl.BlockSpec((1, 1, gw, gw), lambda g, i: (layer, g, 0, 0)),
            pl.BlockSpec((1, 1, 1, gw), lambda g, i: (layer, g, 0, 0)),
            pl.BlockSpec((1, 1, gw), lambda g, i: (layer, 0, g)),
        ],
        out_specs=pl.BlockSpec((tm, gw), lambda g, i: (i, g)),
        compiler_params=_params("parallel", "parallel"),
    )(pooled, w, b, scale)


def _merge_kernel(a1_ref, w1_hbm, a2_ref, w2_hbm, g1_ref, g2_ref, o_ref, *scratch, layer, steps):
    stream1 = _grid_stream(w1_hbm, scratch[:3], layer, steps)
    stream2 = _grid_stream(w2_hbm, scratch[3:], layer, steps)
    w1, w2 = stream1.begin(), stream2.begin()

    def block(rows, cols):
        z1 = jnp.dot(a1_ref[rows, :], w1[:, cols], preferred_element_type=F32)
        z2 = jnp.dot(a2_ref[rows, :], w2[:, cols], preferred_element_type=F32)
        o_ref[rows, cols] = (g1_ref[rows, cols].astype(F32) * z1
                             + g2_ref[rows, cols].astype(F32) * z2).astype(o_ref.dtype)

    _for_blocks(*o_ref.shape, block)
    stream1.finish()
    stream2.finish()


def _merge(o_ssm, w_ssm_out, o_pool, w_pool_out, gates, layer):
    rows, k1 = o_ssm.shape
    k2 = o_pool.shape[1]
    n = w_ssm_out.shape[2]
    tm, tn = min(ROW_TILE, rows), min(COL_TILE, n)
    return pl.pallas_call(
        functools.partial(_merge_kernel, layer=layer, steps=rows // tm),
        out_shape=jax.ShapeDtypeStruct((rows, n), BF16),
        grid=(n // tn, rows // tm),
        in_specs=[
            pl.BlockSpec((tm, k1), lambda j, i: (i, 0)),
            HBM_SPEC,
            pl.BlockSpec((tm, k2), lambda j, i: (i, 0)),
            HBM_SPEC,
            pl.BlockSpec((tm, tn), lambda j, i: (i, j)),
            pl.BlockSpec((tm, tn), lambda j, i: (i, n // tn + j)),
        ],
        out_specs=pl.BlockSpec((tm, tn), lambda j, i: (i, j)),
        scratch_shapes=_stream_scratch(k1, tn, rows // tm) + _stream_scratch(k2, tn, rows // tm),
        compiler_params=_params("arbitrary", "arbitrary"),
    )(o_ssm, w_ssm_out, o_pool, w_pool_out, gates, gates)


def _residual_kernel(a_ref, w_hbm, h_ref, g_ref, o_ref, *scratch, layer, steps):
    stream = _grid_stream(w_hbm, scratch, layer, steps)
    w = stream.begin()

    def block(rows, cols):
        z = jnp.dot(a_ref[rows, :], w[:, cols], preferred_element_type=F32)
        o_ref[rows, cols] = h_ref[rows, cols] + g_ref[0, :, cols] * z

    _for_blocks(*o_ref.shape, block)
    stream.finish()


def _residual(a, w, layer, h, gate, *, tm, tn):
    rows, k = a.shape
    n = w.shape[2]
    rows_per_seq = rows // gate.shape[0]
    tm, tn = min(tm, rows_per_seq), min(tn, n)
    return pl.pallas_call(
        functools.partial(_residual_kernel, layer=layer, steps=rows // tm),
        out_shape=jax.ShapeDtypeStruct((rows, n), F32),
        grid=(n // tn, rows // tm),
        in_specs=[
            pl.BlockSpec((tm, k), lambda j, i: (i, 0)),
            HBM_SPEC,
            pl.BlockSpec((tm, tn), lambda j, i: (i, j)),
            pl.BlockSpec((1, 1, tn), lambda j, i: (i * tm // rows_per_seq, 0, j)),
        ],
        out_specs=pl.BlockSpec((tm, tn), lambda j, i: (i, j)),
        scratch_shapes=_stream_scratch(k, tn, rows // tm),
        input_output_aliases={2: 0},
        compiler_params=_params("arbitrary", "arbitrary"),
    )(a, w, h, gate)


def _up_conv_kernel(a_ref, w_hbm, cwg_ref, cwv_ref, cbg_ref, cbv_ref, o_ref, carry_ref, *scratch, layer, steps):
    _, _, cm, d = a_ref.shape
    tn = o_ref.shape[-1]
    seq_start = pl.program_id(2) == 0
    common = dict(layer=layer, panel=pl.program_id(0), n_panels=pl.num_programs(0),
                  step=pl.program_id(1) * pl.num_programs(2) + pl.program_id(2), steps=steps)
    gate_stream = _WeightStream(w_hbm, *scratch[:3], col_block0=0, **common)
    val_stream = _WeightStream(w_hbm, *scratch[3:], col_block0=pl.num_programs(0), **common)
    wg, wv = gate_stream.begin(), val_stream.begin()

    @pl.when(seq_start)
    def _():
        carry_ref[...] = jnp.zeros(carry_ref.shape, carry_ref.dtype)

    parts = ROW_PARTS if CHUNK * cm >= ROW_TILE else 1
    cp = cm // parts
    chunk_rows = [slice(p * cp, (p + 1) * cp) for p in range(parts)]

    products = {}

    def matmuls(part, cols):
        a = a_ref[0, :, chunk_rows[part], :].reshape(CHUNK * cp, d)
        for name, w in (("gate", wg), ("val", wv)):
            products[name, part, cols.start] = jnp.dot(a, w[:, cols], preferred_element_type=F32)

    for part in range(parts):
        _for_col_blocks(tn, functools.partial(matmuls, part))

    def conv(name, cw_ref, cb_ref, slot, part, cols):
        sub = cols.stop - cols.start
        first_row = lax.broadcasted_iota(jnp.int32, (cp, sub), 0) == 0
        r = products[name, part, cols.start].reshape(CHUNK, cp, sub)
        prev = []
        for t in range(CONV_WIDTH - 1):
            plane = r[CHUNK - (CONV_WIDTH - 1) + t]
            prev.append(jnp.where(first_row, carry_ref[slot + t, 0:1, cols], pltpu.roll(plane, 1, 0)))
            carry_ref[slot + t, :, cols] = jnp.broadcast_to(plane[cp - 1:cp, :], (SUBLANES, sub))
        planes = prev + [r[j] for j in range(CHUNK)]
        outs = []
        for j in range(CHUNK):
            acc = cb_ref[0, :, cols]
            for t in range(CONV_WIDTH):
                acc = acc + cw_ref[0, t:t + 1, cols] * planes[j + t]
            outs.append(acc)
        return outs

    def block(part, cols):
        gate = conv("gate", cwg_ref, cbg_ref, 0, part, cols)
        val = conv("val", cwv_ref, cbv_ref, CONV_WIDTH - 1, part, cols)
        for j in range(CHUNK):
            o_ref[0, j, chunk_rows[part], cols] = (_silu(gate[j]) * val[j]).astype(o_ref.dtype)

    for part in range(parts):
        _for_col_blocks(tn, functools.partial(block, part))
    gate_stream.finish()
    val_stream.finish()


def _up_conv(y4, w_up, conv_w, conv_b, layer):
    bsz, _, nchunk, d = y4.shape
    d_ff = w_up.shape[2] // 2
    cm, tn = min(ROW_TILE // CHUNK, nchunk), min(COL_TILE // 2, d_ff)
    nblk = d_ff // tn
    steps = bsz * (nchunk // cm)
    return pl.pallas_call(
        functools.partial(_up_conv_kernel, layer=layer, steps=steps),
        out_shape=jax.ShapeDtypeStruct((bsz, CHUNK, nchunk, d_ff), BF16),
        grid=(nblk, bsz, nchunk // cm),
        in_specs=[
            pl.BlockSpec((1, CHUNK, cm, d), lambda n, b, c: (b, 0, c, 0)),
            HBM_SPEC,
            pl.BlockSpec((1, CONV_WIDTH, tn), lambda n, b, c: (layer, 0, n)),
            pl.BlockSpec((1, CONV_WIDTH, tn), lambda n, b, c: (layer, 0, nblk + n)),
            pl.BlockSpec((1, 1, tn), lambda n, b, c: (layer, 0, n)),
            pl.BlockSpec((1, 1, tn), lambda n, b, c: (layer, 0, nblk + n)),
        ],
        out_specs=pl.BlockSpec((1, CHUNK, cm, tn), lambda n, b, c: (b, 0, c, n)),
        scratch_shapes=([pltpu.VMEM((2 * (CONV_WIDTH - 1), SUBLANES, tn), F32)]
                        + _stream_scratch(d, tn, steps) + _stream_scratch(d, tn, steps)),
        compiler_params=_params("arbitrary", "arbitrary", "arbitrary"),
    )(y4, w_up, conv_w, conv_w, conv_b, conv_b)


def kernel(x, c, w_cond, b_cond, ada_table, norm1_g, norm2_g, w_in, ssm_a_re, ssm_a_im, ssm_log_dt, ssm_b_re, ssm_b_im, ssm_c_re, ssm_c_im, ssm_d, w_glu, b_glu, w_pool, b_pool, pool_scale, w_ssm_out, w_pool_out, w_o, w_up, conv_w, conv_b, w_down, final_g):
    bsz, seq, d = x.shape
    depth = w_in.shape[0]
    nchunk = seq // CHUNK
    rows = bsz * seq
    ssm_width = w_glu.shape[2]
    pool_width = pool_scale.shape[1]
    assert seq % CHUNK == 0 and ssm_width % SLICE_CH == 0

    c_pad = jnp.pad(c, ((0, SUBLANES - bsz), (0, 0)))
    mods = _cond(c_pad, w_cond, b_cond, ada_table)[:, :bsz].reshape(depth, bsz, N_MOD, 1, d)
    kt, p, q, dre, dim = _s5_prep(ssm_a_re, ssm_a_im, ssm_log_dt, ssm_b_re, ssm_b_im, ssm_c_re, ssm_c_im)
    d_skip = ssm_d.reshape(depth, 1, ssm_width)
    b_glu = b_glu.reshape(depth, 1, ssm_width)
    b_pool = b_pool[:, :, None, :]
    pool_scale = pool_scale.reshape(depth, 1, pool_width)
    conv_b = conv_b.reshape(depth, 1, -1)

    h = None
    for l in range(depth):
        shift1, scale1, gate1, shift2, scale2, gate2 = (mods[l, :, i] for i in range(N_MOD))
        if l == 0:
            y, h = _norm(x, norm1_g[l], scale1, shift1, in_tokens=True, copy_out=True)
            h = h.reshape(rows, d)
        else:
            y = _norm(h.reshape(bsz, CHUNK, nchunk, d), norm1_g[l], scale1, shift1)
        y = y.reshape(rows, d)
        u = _proj(y, w_in, l, 0, ssm_width + pool_width, sigmoid=False)
        gates = _proj(y, w_in, l, ssm_width + pool_width, 2 * d, sigmoid=True)
        u4 = u.reshape(bsz, CHUNK, nchunk, -1)
        y_ssm = _s5(u4, kt, p, q, dre, dim, d_skip, l).reshape(rows, ssm_width)
        o_ssm = _glu(y_ssm, w_glu, b_glu, l)
        pooled = _pool(u4, ssm_width, pool_width).reshape(rows, pool_width)
        o_pool = _pool_mix(pooled, w_pool, b_pool, pool_scale, l)
        merged = _merge(o_ssm, w_ssm_out, o_pool, w_pool_out, gates, l)
        h = _residual(merged, w_o, l, h, gate1, tm=ROW_TILE, tn=COL_TILE)

        y = _norm(h.reshape(bsz, CHUNK, nchunk, d), norm2_g[l], scale2, shift2)
        act = _up_conv(y, w_up, conv_w, conv_b, l).reshape(rows, -1)
        h = _residual(act, w_down, l, h, gate2, tm=ROW_TILE // 2, tn=COL_TILE // 2)

    return _norm(h.reshape(bsz, CHUNK, nchunk, d), final_g, None, None, out_tokens=True, out_dtype=x.dtype)
```

```python
import functools

import jax
import jax.numpy as jnp
from jax import lax
from jax.experimental import pallas as pl
from jax.experimental.pallas import tpu as pltpu

F32 = jnp.float32
BF16 = jnp.bfloat16

CHUNK = 16
SSM_GROUP = 16
SSM_STATE = 64
SLICE_CH = 128
SLICE_GROUPS = SLICE_CH // SSM_GROUP
SLICE_STATES = SLICE_GROUPS * SSM_STATE
POOL_WINDOWS = (2, 4, 8, 16)
CONV_WIDTH = 3
N_MOD = 6
RMS_EPS = 1e-6
LAM_RE_MAX = -1e-4
SUBLANES = 8
LANES = 128
MXU_COLS = 256
V7X_VMEM_LIMIT = 56 * 1024 * 1024
ROW_TILE = 1024
COL_TILE = 1024
MAX_WEIGHT_CHUNKS = 8
ROW_PARTS = 2


def _params(*sem):
    return pltpu.CompilerParams(dimension_semantics=sem, vmem_limit_bytes=V7X_VMEM_LIMIT)


def _shift_rows(x, sh):
    rolled = pltpu.roll(x, sh, 0)
    idx = lax.broadcasted_iota(jnp.int32, x.shape, 0)
    return jnp.where(idx < sh, jnp.zeros_like(x), rolled)


class _WeightStream:
    def __init__(self, w_hbm, wbf_ref, stage_ref, sem, *, layer, col_block0, panel, n_panels, step, steps):
        self.w_hbm, self.wbf, self.stage, self.sem = w_hbm, wbf_ref, stage_ref, sem
        self.layer, self.col_block0 = layer, col_block0
        self.panel, self.step = panel, step
        _, self.k, self.tn = wbf_ref.shape
        self.kc = stage_ref.shape[1]
        self.n_chunks = self.k // self.kc
        self.lag = 1 if self.n_chunks + 1 <= steps else 0
        assert self.n_chunks + self.lag <= steps
        has_next = panel + 1 < n_panels
        self.starting = jnp.logical_and(has_next, step < self.n_chunks)
        self.finishing = jnp.logical_and(has_next, jnp.logical_and(step >= self.lag, step < self.n_chunks + self.lag))

    def _rows(self, chunk):
        start = chunk * self.kc
        return pl.ds(start if isinstance(start, int) else pl.multiple_of(start, self.kc), self.kc)

    def _copy(self, panel, chunk, slot):
        col = (self.col_block0 + panel) * self.tn
        cols = pl.ds(col if isinstance(col, int) else pl.multiple_of(col, self.tn), self.tn)
        return pltpu.make_async_copy(self.w_hbm.at[self.layer, self._rows(chunk), cols],
                                     self.stage.at[slot], self.sem.at[slot])

    def _cast(self, wbf_slot, chunk, slot):
        self.wbf[wbf_slot, self._rows(chunk), :] = self.stage[slot].astype(BF16)

    def begin(self):
        @pl.when(jnp.logical_and(self.panel == 0, self.step == 0))
        def _():
            self._copy(0, 0, 0).start()
            for c in range(self.n_chunks):
                if c + 1 < self.n_chunks:
                    self._copy(0, c + 1, (c + 1) % 2).start()
                self._copy(0, c, c % 2).wait()
                self._cast(0, c, c % 2)

        @pl.when(self.starting)
        def _():
            self._copy(self.panel + 1, self.step, self.step % 2).start()

        return self.wbf.at[self.panel % 2]

    def finish(self):
        @pl.when(self.finishing)
        def _():
            chunk = self.step - self.lag
            self._copy(self.panel + 1, chunk, chunk % 2).wait()
            self._cast((self.panel + 1) % 2, chunk, chunk % 2)


def _stream_scratch(k, tn, steps):
    n_chunks = 1
    while 2 * n_chunks <= min(steps - 1, MAX_WEIGHT_CHUNKS):
        n_chunks *= 2
    return [pltpu.VMEM((2, k, tn), BF16), pltpu.VMEM((2, k // n_chunks, tn), F32), pltpu.SemaphoreType.DMA((2,))]


HBM_SPEC = pl.BlockSpec(memory_space=pl.ANY)


def _sigmoid(x):
    return 0.5 * jnp.tanh(0.5 * x) + 0.5


def _silu(x):
    h = 0.5 * x
    return h * jnp.tanh(h) + h


def _for_col_blocks(width, body):
    sub = min(MXU_COLS, width)
    for j in range(width // sub):
        body(slice(j * sub, (j + 1) * sub))


def _for_blocks(rows, width, body):
    parts = ROW_PARTS if rows >= ROW_TILE else 1
    part = rows // parts
    for r in range(parts):
        _for_col_blocks(width, functools.partial(body, slice(r * part, (r + 1) * part)))


def _cond_kernel(c_ref, w_ref, b_ref, ada_ref, o_ref):
    a = jax.nn.silu(c_ref[...]).astype(BF16)
    z = jnp.dot(a, w_ref[...].astype(BF16), preferred_element_type=F32) + b_ref[...]
    o_ref[...] = z[None] + ada_ref[...]


def _cond(c_pad, w_cond, b_cond, ada_table):
    rows, d = c_pad.shape
    n = w_cond.shape[1]
    depth = ada_table.shape[0]
    tn = min(COL_TILE // 2, n)
    return pl.pallas_call(
        _cond_kernel,
        out_shape=jax.ShapeDtypeStruct((depth, rows, n), F32),
        grid=(n // tn,),
        in_specs=[
            pl.BlockSpec((rows, d), lambda i: (0, 0)),
            pl.BlockSpec((d, tn), lambda i: (0, i)),
            pl.BlockSpec((1, tn), lambda i: (0, i)),
            pl.BlockSpec((depth, 1, tn), lambda i: (0, 0, i)),
        ],
        out_specs=pl.BlockSpec((depth, rows, tn), lambda i: (0, 0, i)),
        compiler_params=_params("arbitrary"),
    )(c_pad, w_cond, b_cond.reshape(1, n), ada_table.reshape(depth, 1, n))


def _norm_kernel(*refs, modulated, in_tokens, out_tokens, copy_out):
    refs = list(refs)
    x_ref, g_ref = refs[:2]
    scale_ref, shift_ref = refs[2:4] if modulated else (None, None)
    outs = refs[4 if modulated else 2:]
    o_ref = outs[0]
    copy_ref = outs[1] if copy_out else None
    stage_ref = outs[-1] if (in_tokens or out_tokens) else None
    tc = (x_ref.shape[1] // CHUNK) if in_tokens else x_ref.shape[2]
    lane_tiles = x_ref.shape[-1] // LANES
    plane_rows = lambda j: pl.ds(j, tc, stride=CHUNK)
    if in_tokens:
        for k in range(lane_tiles):
            stage_ref[k] = x_ref[0, :, k * LANES:(k + 1) * LANES]
    for j in range(CHUNK):
        if in_tokens:
            x = jnp.concatenate([stage_ref[k, plane_rows(j), :] for k in range(lane_tiles)], axis=-1)
        else:
            x = x_ref[0, j]
        y = x * lax.rsqrt(jnp.mean(x * x, axis=-1, keepdims=True) + RMS_EPS) * g_ref[...]
        if modulated:
            y = y * (1.0 + scale_ref[0]) + shift_ref[0]
        if out_tokens:
            for k in range(lane_tiles):
                stage_ref[k, plane_rows(j), :] = y[:, k * LANES:(k + 1) * LANES]
        else:
            o_ref[0, j] = y.astype(o_ref.dtype)
        if copy_out:
            copy_ref[0, j] = x
    if out_tokens:
        for k in range(lane_tiles):
            o_ref[0, :, k * LANES:(k + 1) * LANES] = stage_ref[k].astype(o_ref.dtype)


def _norm(x, g, scale, shift, *, in_tokens=False, out_tokens=False, copy_out=False, out_dtype=BF16):
    if in_tokens:
        bsz, seq, d = x.shape
        nchunk = seq // CHUNK
    else:
        bsz, _, nchunk, d = x.shape
    tc = min(16 if (in_tokens or out_tokens) else 64, nchunk)
    tok_spec = pl.BlockSpec((1, tc * CHUNK, d), lambda b, c: (b, c, 0))
    pos_spec = pl.BlockSpec((1, CHUNK, tc, d), lambda b, c: (b, 0, c, 0))
    modulated = scale is not None
    in_specs = [tok_spec if in_tokens else pos_spec, pl.BlockSpec((1, d), lambda b, c: (0, 0))]
    args = [x, g.reshape(1, d)]
    if modulated:
        mod_spec = pl.BlockSpec((1, 1, d), lambda b, c: (b, 0, 0))
        in_specs += [mod_spec, mod_spec]
        args += [scale, shift]
    pos_shape = (bsz, CHUNK, nchunk, d)
    out_shape = [jax.ShapeDtypeStruct((bsz, nchunk * CHUNK, d) if out_tokens else pos_shape, out_dtype)]
    out_specs = [tok_spec if out_tokens else pos_spec]
    if copy_out:
        out_shape.append(jax.ShapeDtypeStruct(pos_shape, F32))
        out_specs.append(pos_spec)
    staged = in_tokens or out_tokens
    scratch = [pltpu.VMEM((d // LANES, tc * CHUNK, LANES), F32)] if staged else []
    outs = pl.pallas_call(
        functools.partial(_norm_kernel, modulated=modulated, in_tokens=in_tokens, out_tokens=out_tokens,
                          copy_out=copy_out),
        out_shape=out_shape,
        grid=(bsz, nchunk // tc),
        in_specs=in_specs,
        out_specs=out_specs,
        scratch_shapes=scratch,
        compiler_params=_params("parallel", "parallel"),
    )(*args)
    return outs if copy_out else outs[0]


def _grid_stream(w_hbm, scratch, layer, steps, col_block0=0):
    return _WeightStream(w_hbm, *scratch, layer=layer, col_block0=col_block0, panel=pl.program_id(0),
                         n_panels=pl.num_programs(0), step=pl.program_id(1), steps=steps)


def _proj_kernel(a_ref, w_hbm, o_ref, *scratch, sigmoid, layer, steps, col_block0):
    stream = _grid_stream(w_hbm, scratch, layer, steps, col_block0)
    w = stream.begin()

    def block(rows, cols):
        z = jnp.dot(a_ref[rows, :], w[:, cols], preferred_element_type=F32)
        o_ref[rows, cols] = (_sigmoid(z) if sigmoid else z).astype(o_ref.dtype)

    _for_blocks(*o_ref.shape, block)
    stream.finish()


def _proj(a, w, layer, col0, n, *, sigmoid):
    rows, k = a.shape
    tm, tn = min(ROW_TILE, rows), min(COL_TILE, n)
    return pl.pallas_call(
        functools.partial(_proj_kernel, sigmoid=sigmoid, layer=layer, steps=rows // tm, col_block0=col0 // tn),
        out_shape=jax.ShapeDtypeStruct((rows, n), BF16),
        grid=(n // tn, rows // tm),
        in_specs=[pl.BlockSpec((tm, k), lambda j, i: (i, 0)), HBM_SPEC],
        out_specs=pl.BlockSpec((tm, tn), lambda j, i: (i, j)),
        scratch_shapes=_stream_scratch(k, tn, rows // tm),
        compiler_params=_params("arbitrary", "arbitrary"),
    )(a, w)


def _discretise(a_re, a_im, log_dt):
    lam_re = jnp.minimum(a_re, LAM_RE_MAX)
    lam_im = a_im
    dt = jnp.exp(log_dt)
    mag = jnp.exp(lam_re * dt)
    abar_re = mag * jnp.cos(lam_im * dt)
    abar_im = mag * jnp.sin(lam_im * dt)
    den = lam_re * lam_re + lam_im * lam_im
    x_re = abar_re - 1.0
    f_re = (x_re * lam_re + abar_im * lam_im) / den
    f_im = (abar_im * lam_re - x_re * lam_im) / den
    return abar_re, abar_im, f_re, f_im


def _split_bf16(x):
    hi = x.astype(BF16)
    return hi, (x - hi.astype(F32)).astype(BF16)


def _dot_split(a, b):
    a_hi, a_lo = a
    b_hi, b_lo = b
    dot = functools.partial(jnp.dot, preferred_element_type=F32)
    return dot(a_hi, b_hi) + (dot(a_hi, b_lo) + dot(a_lo, b_hi))


def _s5_prep_kernel(rowp_ref, bt_ref, ct_ref, kt_ref, p_ref, q_ref, dre_ref, dim_ref):
    ns, ch = SLICE_STATES, SLICE_CH
    ar, ai, f_re, f_im = _discretise(rowp_ref[0, 0, 0:1, :], rowp_ref[0, 0, 1:2, :], rowp_ref[0, 0, 2:3, :])
    bt_re = jnp.concatenate([bt_ref[0, 0, 0]] * SLICE_GROUPS, axis=0)
    bt_im = jnp.concatenate([bt_ref[0, 0, 1]] * SLICE_GROUPS, axis=0)
    same_group = (lax.broadcasted_iota(jnp.int32, (ch, ns), 0) // SSM_GROUP
                  == lax.broadcasted_iota(jnp.int32, (ch, ns), 1) // SSM_STATE)
    e_re = jnp.where(same_group, f_re * bt_re - f_im * bt_im, 0.0)
    e_im = jnp.where(same_group, f_re * bt_im + f_im * bt_re, 0.0)
    same_group_t = (lax.broadcasted_iota(jnp.int32, (ns, ch), 0) // SSM_STATE
                    == lax.broadcasted_iota(jnp.int32, (ns, ch), 1) // SSM_GROUP)
    c_re = jnp.where(same_group_t, ct_ref[0, 0, 0], 0.0)
    c_im = jnp.where(same_group_t, ct_ref[0, 0, 1], 0.0)
    c_re_split, c_im_split = _split_bf16(c_re), _split_bf16(c_im)
    for t in range(CHUNK):
        j = CHUNK - 1 - t
        e_re_split, e_im_split = _split_bf16(e_re), _split_bf16(e_im)
        p_ref[0, 0, j * ch:(j + 1) * ch, 0:ns] = e_re_split[0]
        p_ref[0, 0, j * ch:(j + 1) * ch, ns:2 * ns] = e_im_split[0]
        kt_ref[0, 0, t] = (_dot_split(e_re_split, c_re_split) - _dot_split(e_im_split, c_im_split)).astype(BF16)
        e_re, e_im = e_re * ar - e_im * ai, e_re * ai + e_im * ar

    d_re, d_im = ar, ai
    for _ in range(CHUNK.bit_length() - 1):
        d_re, d_im = d_re * d_re - d_im * d_im, 2.0 * d_re * d_im
    p_re, p_im = d_re, d_im
    for r in range(SUBLANES):
        dre_ref[0, 0, r:r + 1, :] = p_re
        dim_ref[0, 0, r:r + 1, :] = p_im
        p_re, p_im = p_re * d_re - p_im * d_im, p_re * d_im + p_im * d_re

    arc = jnp.transpose(jnp.broadcast_to(ar, (ch, ns)))
    aic = jnp.transpose(jnp.broadcast_to(ai, (ch, ns)))
    w_re, w_im = c_re, c_im
    for i in range(CHUNK):
        w_re, w_im = w_re * arc - w_im * aic, w_re * aic + w_im * arc
        q_ref[0, 0, 0:ns, i * ch:(i + 1) * ch] = w_re.astype(BF16)
        q_ref[0, 0, ns:2 * ns, i * ch:(i + 1) * ch] = (-w_im).astype(BF16)


def _s5_prep(a_re, a_im, log_dt, b_re, b_im, c_re, c_im):
    depth, groups, nst = a_re.shape
    nsl = groups // SLICE_GROUPS
    ns, ch = SLICE_STATES, SLICE_CH
    dt_b = jnp.broadcast_to(log_dt[:, :, None], a_re.shape)
    rows = jnp.stack([a_re, a_im, dt_b], axis=2).reshape(depth, nsl, SLICE_GROUPS, 3, nst)
    rows = jnp.swapaxes(rows, 2, 3).reshape(depth, nsl, 3, ns)
    rowp = jnp.pad(rows, ((0, 0), (0, 0), (0, SUBLANES - 3), (0, 0)))
    bt = jnp.stack([b_re, b_im], axis=1).reshape(depth, 2, nsl, ns, SSM_GROUP)
    bt = jnp.transpose(bt, (0, 2, 1, 4, 3))
    ct = jnp.stack([c_re, c_im], axis=1).reshape(depth, 2, nsl, SLICE_GROUPS, SSM_GROUP, nst)
    ct = jnp.transpose(ct, (0, 2, 1, 3, 5, 4)).reshape(depth, nsl, 2, ns, SSM_GROUP)
    ct = jnp.tile(ct, (1, 1, 1, 1, SLICE_GROUPS))
    out_shapes = (
        jax.ShapeDtypeStruct((depth, nsl, CHUNK, ch, ch), BF16),
        jax.ShapeDtypeStruct((depth, nsl, CHUNK * ch, 2 * ns), BF16),
        jax.ShapeDtypeStruct((depth, nsl, 2 * ns, CHUNK * ch), BF16),
        jax.ShapeDtypeStruct((depth, nsl, SUBLANES, ns), F32),
        jax.ShapeDtypeStruct((depth, nsl, SUBLANES, ns), F32),
    )
    return pl.pallas_call(
        _s5_prep_kernel,
        out_shape=out_shapes,
        grid=(depth, nsl),
        in_specs=[
            pl.BlockSpec((1, 1, SUBLANES, ns), lambda l, s: (l, s, 0, 0)),
            pl.BlockSpec((1, 1, 2, SSM_GROUP, ns), lambda l, s: (l, s, 0, 0, 0)),
            pl.BlockSpec((1, 1, 2, ns, ch), lambda l, s: (l, s, 0, 0, 0)),
        ],
        out_specs=(
            pl.BlockSpec((1, 1, CHUNK, ch, ch), lambda l, s: (l, s, 0, 0, 0)),
            pl.BlockSpec((1, 1, CHUNK * ch, 2 * ns), lambda l, s: (l, s, 0, 0)),
            pl.BlockSpec((1, 1, 2 * ns, CHUNK * ch), lambda l, s: (l, s, 0, 0)),
            pl.BlockSpec((1, 1, SUBLANES, ns), lambda l, s: (l, s, 0, 0)),
            pl.BlockSpec((1, 1, SUBLANES, ns), lambda l, s: (l, s, 0, 0)),
        ),
        compiler_params=_params("parallel", "parallel"),
    )(rowp, bt, ct)


def _chunk_scan_prev(x_re, x_im, dre_ref, dim_ref):
    nrows, ns = x_re.shape
    nblk = nrows // SUBLANES
    x_re = x_re.reshape(nblk, SUBLANES, ns)
    x_im = x_im.reshape(nblk, SUBLANES, ns)
    row = lax.broadcasted_iota(jnp.int32, (SUBLANES, ns), 0)
    sh = 1
    while sh < SUBLANES:
        d_re = jnp.where(row >= sh, dre_ref[0, 0, sh - 1:sh, :], 0.0)
        d_im = jnp.where(row >= sh, dim_ref[0, 0, sh - 1:sh, :], 0.0)
        s_re, s_im = pltpu.roll(x_re, sh, 1), pltpu.roll(x_im, sh, 1)
        x_re, x_im = x_re + d_re * s_re - d_im * s_im, x_im + d_re * s_im + d_im * s_re
        sh *= 2
    pw_re, pw_im = dre_ref[0, 0], dim_ref[0, 0]
    first_row = row == 0
    c_re = c_im = jnp.zeros((SUBLANES, ns), F32)
    prev_re, prev_im = [], []
    for blk in range(nblk):
        b_re = x_re[blk] + pw_re * c_re - pw_im * c_im
        b_im = x_im[blk] + pw_re * c_im + pw_im * c_re
        prev_re.append(jnp.where(first_row, c_re, pltpu.roll(b_re, 1, 0)))
        prev_im.append(jnp.where(first_row, c_im, pltpu.roll(b_im, 1, 0)))
        c_re = jnp.broadcast_to(b_re[SUBLANES - 1:SUBLANES], pw_re.shape)
        c_im = jnp.broadcast_to(b_im[SUBLANES - 1:SUBLANES], pw_re.shape)
    return jnp.concatenate(prev_re, axis=0), jnp.concatenate(prev_im, axis=0)


def _s5_kernel(u_ref, kt_ref, p_ref, q_ref, dre_ref, dim_ref, dskip_ref, o_ref, toep_ref, y_ref):
    ch, ns = SLICE_CH, SLICE_STATES
    per_blk = MXU_COLS // ch
    nblk = CHUNK // per_blk

    @pl.when(pl.program_id(1) == 0)
    def _():
        for i in range(CHUNK):
            last_j = (i // per_blk + 1) * per_blk
            for j in range(last_j):
                blk = kt_ref[0, 0, i - j] if j <= i else jnp.zeros((ch, ch), BF16)
                toep_ref[j * ch:(j + 1) * ch, i * ch:(i + 1) * ch] = blk

    u_all = jnp.concatenate([u_ref[0, j] for j in range(CHUNK)], axis=-1)
    x_loc = jnp.dot(u_all, p_ref[0, 0], preferred_element_type=F32)
    for t in range(nblk):
        cols = slice(t * MXU_COLS, (t + 1) * MXU_COLS)
        k_rows = (t + 1) * MXU_COLS
        y_ref[:, cols] = jnp.dot(u_all[:, :k_rows], toep_ref[:k_rows, cols], preferred_element_type=F32)
    x_re, x_im = _chunk_scan_prev(x_loc[:, :ns], x_loc[:, ns:], dre_ref, dim_ref)
    x_prev = jnp.concatenate([x_re, x_im], axis=-1).astype(BF16)
    for t in range(nblk):
        cols = slice(t * MXU_COLS, (t + 1) * MXU_COLS)
        y_blk = y_ref[:, cols] + jnp.dot(x_prev, q_ref[0, 0, :, cols], preferred_element_type=F32)
        for s in range(per_blk):
            i = t * per_blk + s
            y = y_blk[:, s * ch:(s + 1) * ch] + u_ref[0, i].astype(F32) * dskip_ref[0]
            o_ref[0, i] = jax.nn.gelu(y).astype(o_ref.dtype)


def _s5(u4, kt, p, q, dre, dim, d_skip, layer):
    bsz, _, nchunk, _ = u4.shape
    ch, ns = SLICE_CH, SLICE_STATES
    nsl = kt.shape[1]
    assert nchunk % SUBLANES == 0
    return pl.pallas_call(
        _s5_kernel,
        out_shape=jax.ShapeDtypeStruct((bsz, CHUNK, nchunk, nsl * ch), BF16),
        grid=(nsl, bsz),
        in_specs=[
            pl.BlockSpec((1, CHUNK, nchunk, ch), lambda s, b: (b, 0, 0, s)),
            pl.BlockSpec((1, 1, CHUNK, ch, ch), lambda s, b: (layer, s, 0, 0, 0)),
            pl.BlockSpec((1, 1, CHUNK * ch, 2 * ns), lambda s, b: (layer, s, 0, 0)),
            pl.BlockSpec((1, 1, 2 * ns, CHUNK * ch), lambda s, b: (layer, s, 0, 0)),
            pl.BlockSpec((1, 1, SUBLANES, ns), lambda s, b: (layer, s, 0, 0)),
            pl.BlockSpec((1, 1, SUBLANES, ns), lambda s, b: (layer, s, 0, 0)),
            pl.BlockSpec((1, 1, ch), lambda s, b: (layer, 0, s)),
        ],
        out_specs=pl.BlockSpec((1, CHUNK, nchunk, ch), lambda s, b: (b, 0, 0, s)),
        scratch_shapes=[pltpu.VMEM((CHUNK * ch, CHUNK * ch), BF16), pltpu.VMEM((nchunk, CHUNK * ch), F32)],
        compiler_params=_params("arbitrary", "arbitrary"),
    )(u4, kt, p, q, dre, dim, d_skip)


def _glu_kernel(a_ref, w_hbm, b_ref, y_ref, o_ref, *scratch, layer, steps):
    stream = _grid_stream(w_hbm, scratch, layer, steps)
    w = stream.begin()

    def block(rows, cols):
        z = jnp.dot(a_ref[rows, :], w[:, cols], preferred_element_type=F32) + b_ref[0, :, cols]
        o_ref[rows, cols] = (y_ref[rows, cols].astype(F32) * _sigmoid(z)).astype(o_ref.dtype)

    _for_blocks(*o_ref.shape, block)
    stream.finish()


def _glu(y, w, b, layer):
    rows, k = y.shape
    n = w.shape[2]
    tm, tn = min(2 * ROW_TILE, rows), min(COL_TILE, n)
    return pl.pallas_call(
        functools.partial(_glu_kernel, layer=layer, steps=rows // tm),
        out_shape=jax.ShapeDtypeStruct((rows, n), BF16),
        grid=(n // tn, rows // tm),
        in_specs=[
            pl.BlockSpec((tm, k), lambda j, i: (i, 0)),
            HBM_SPEC,
            pl.BlockSpec((1, 1, tn), lambda j, i: (layer, 0, j)),
            pl.BlockSpec((tm, tn), lambda j, i: (i, j)),
        ],
        out_specs=pl.BlockSpec((tm, tn), lambda j, i: (i, j)),
        scratch_shapes=_stream_scratch(k, tn, rows // tm),
        compiler_params=_params("arbitrary", "arbitrary"),
    )(y, w, b, y)


def _pool_kernel(v_ref, w_ref, b_ref, s_ref, o_ref, pooled_ref):
    nchunk = v_ref.shape[2]
    group, tile = pl.program_id(1), pl.program_id(2)
    first_row = lax.broadcasted_iota(jnp.int32, (nchunk, 1), 0) == 0

    def pooled(win):
        v = [v_ref[0, j].astype(F32) for j in range(CHUNK)]
        prefix = [v[0]]
        for j in range(1, CHUNK):
            prefix.append(prefix[-1] + v[j])
        for j in range(CHUNK):
            if j >= win:
                wsum = prefix[j] - prefix[j - win]
            elif j == win - 1:
                wsum = prefix[j]
            else:
                wsum = prefix[j] + _shift_rows(prefix[CHUNK - 1] - prefix[j - win + CHUNK], 1)
            count = jnp.where(first_row, float(min(j + 1, win)), float(win))
            pooled_ref[tile, j] = (wsum / count - v[j]).astype(pooled_ref.dtype)

    for g, win in enumerate(POOL_WINDOWS):
        pl.when(group == g)(functools.partial(pooled, win))

    @pl.when(tile == pl.num_programs(2) - 1)
    def _():
        w = w_ref[0, 0].astype(BF16)
        for j in range(CHUNK):
            a = jnp.concatenate([pooled_ref[k, j] for k in range(pooled_ref.shape[0])], axis=-1)
            z = jnp.dot(a, w, preferred_element_type=F32)
            o_ref[0, j] = ((z + b_ref[0, 0]) * s_ref[0]).astype(o_ref.dtype)


def _pool(u4, col0, w, b, scale, layer):
    bsz, _, nchunk, _ = u4.shape
    ngroups, gw = w.shape[1], w.shape[2]
    lane_tiles = gw // LANES
    return pl.pallas_call(
        _pool_kernel,
        out_shape=jax.ShapeDtypeStruct((bsz, CHUNK, nchunk, ngroups * gw), BF16),
        grid=(bsz, ngroups, lane_tiles),
        in_specs=[
            pl.BlockSpec((1, CHUNK, nchunk, LANES), lambda s, g, t: (s, 0, 0, col0 // LANES + g * lane_tiles + t)),
            pl.BlockSpec((1, 1, gw, gw), lambda s, g, t: (layer, g, 0, 0)),
            pl.BlockSpec((1, 1, 1, gw), lambda s, g, t: (layer, g, 0, 0)),
            pl.BlockSpec((1, 1, gw), lambda s, g, t: (layer, 0, g)),
        ],
        out_specs=pl.BlockSpec((1, CHUNK, nchunk, gw), lambda s, g, t: (s, 0, 0, g)),
        scratch_shapes=[pltpu.VMEM((lane_tiles, CHUNK, nchunk, LANES), BF16)],
        compiler_params=_params("arbitrary", "arbitrary", "arbitrary"),
    )(u4, w, b, scale)


def _merge_kernel(a1_ref, w1_hbm, a2_ref, w2_hbm, g1_ref, g2_ref, o_ref, *scratch, layer, steps):
    stream1 = _grid_stream(w1_hbm, scratch[:3], layer, steps)
    stream2 = _grid_stream(w2_hbm, scratch[3:], layer, steps)
    w1, w2 = stream1.begin(), stream2.begin()

    def block(rows, cols):
        z1 = jnp.dot(a1_ref[rows, :], w1[:, cols], preferred_element_type=F32)
        z2 = jnp.dot(a2_ref[rows, :], w2[:, cols], preferred_element_type=F32)
        o_ref[rows, cols] = (g1_ref[rows, cols].astype(F32) * z1
                             + g2_ref[rows, cols].astype(F32) * z2).astype(o_ref.dtype)

    _for_blocks(*o_ref.shape, block)
    stream1.finish()
    stream2.finish()


def _merge(o_ssm, w_ssm_out, o_pool, w_pool_out, gates, layer):
    rows, k1 = o_ssm.shape
    k2 = o_pool.shape[1]
    n = w_ssm_out.shape[2]
    tm, tn = min(ROW_TILE, rows), min(COL_TILE, n)
    return pl.pallas_call(
        functools.partial(_merge_kernel, layer=layer, steps=rows // tm),
        out_shape=jax.ShapeDtypeStruct((rows, n), BF16),
        grid=(n // tn, rows // tm),
        in_specs=[
            pl.BlockSpec((tm, k1), lambda j, i: (i, 0)),
            HBM_SPEC,
            pl.BlockSpec((tm, k2), lambda j, i: (i, 0)),
            HBM_SPEC,
            pl.BlockSpec((tm, tn), lambda j, i: (i, j)),
            pl.BlockSpec((tm, tn), lambda j, i: (i, n // tn + j)),
        ],
        out_specs=pl.BlockSpec((tm, tn), lambda j, i: (i, j)),
        scratch_shapes=_stream_scratch(k1, tn, rows // tm) + _stream_scratch(k2, tn, rows // tm),
        compiler_params=_params("arbitrary", "arbitrary"),
    )(o_ssm, w_ssm_out, o_pool, w_pool_out, gates, gates)


def _residual_kernel(a_ref, w_hbm, h_ref, g_ref, o_ref, *scratch, layer, steps):
    stream = _grid_stream(w_hbm, scratch, layer, steps)
    w = stream.begin()

    def block(rows, cols):
        z = jnp.dot(a_ref[rows, :], w[:, cols], preferred_element_type=F32)
        o_ref[rows, cols] = h_ref[rows, cols] + g_ref[0, :, cols] * z

    _for_blocks(*o_ref.shape, block)
    stream.finish()


def _residual(a, w, layer, h, gate, *, tm, tn):
    rows, k = a.shape
    n = w.shape[2]
    rows_per_seq = rows // gate.shape[0]
    tm, tn = min(tm, rows_per_seq), min(tn, n)
    return pl.pallas_call(
        functools.partial(_residual_kernel, layer=layer, steps=rows // tm),
        out_shape=jax.ShapeDtypeStruct((rows, n), F32),
        grid=(n // tn, rows // tm),
        in_specs=[
            pl.BlockSpec((tm, k), lambda j, i: (i, 0)),
            HBM_SPEC,
            pl.BlockSpec((tm, tn), lambda j, i: (i, j)),
            pl.BlockSpec((1, 1, tn), lambda j, i: (i * tm // rows_per_seq, 0, j)),
        ],
        out_specs=pl.BlockSpec((tm, tn), lambda j, i: (i, j)),
        scratch_shapes=_stream_scratch(k, tn, rows // tm),
        input_output_aliases={2: 0},
        compiler_params=_params("arbitrary", "arbitrary"),
    )(a, w, h, gate)


def _up_conv_kernel(a_ref, w_hbm, cwg_ref, cwv_ref, cbg_ref, cbv_ref, o_ref, carry_ref, *scratch, layer, steps):
    _, _, cm, d = a_ref.shape
    tn = o_ref.shape[-1]
    seq_start = pl.program_id(2) == 0
    common = dict(layer=layer, panel=pl.program_id(0), n_panels=pl.num_programs(0),
                  step=pl.program_id(1) * pl.num_programs(2) + pl.program_id(2), steps=steps)
    gate_stream = _WeightStream(w_hbm, *scratch[:3], col_block0=0, **common)
    val_stream = _WeightStream(w_hbm, *scratch[3:], col_block0=pl.num_programs(0), **common)
    wg, wv = gate_stream.begin(), val_stream.begin()

    @pl.when(seq_start)
    def _():
        carry_ref[...] = jnp.zeros(carry_ref.shape, carry_ref.dtype)

    parts = ROW_PARTS if CHUNK * cm >= ROW_TILE else 1
    cp = cm // parts
    chunk_rows = [slice(p * cp, (p + 1) * cp) for p in range(parts)]

    products = {}

    def matmuls(part, cols):
        a = a_ref[0, :, chunk_rows[part], :].reshape(CHUNK * cp, d)
        for name, w in (("gate", wg), ("val", wv)):
            products[name, part, cols.start] = jnp.dot(a, w[:, cols], preferred_element_type=F32)

    for part in range(parts):
        _for_col_blocks(tn, functools.partial(matmuls, part))

    def conv(name, cw_ref, cb_ref, slot, part, cols):
        sub = cols.stop - cols.start
        first_row = lax.broadcasted_iota(jnp.int32, (cp, sub), 0) == 0
        r = products[name, part, cols.start].reshape(CHUNK, cp, sub)
        prev = []
        for t in range(CONV_WIDTH - 1):
            plane = r[CHUNK - (CONV_WIDTH - 1) + t]
            prev.append(jnp.where(first_row, carry_ref[slot + t, 0:1, cols], pltpu.roll(plane, 1, 0)))
            carry_ref[slot + t, :, cols] = jnp.broadcast_to(plane[cp - 1:cp, :], (SUBLANES, sub))
        planes = prev + [r[j] for j in range(CHUNK)]
        outs = []
        for j in range(CHUNK):
            acc = cb_ref[0, :, cols]
            for t in range(CONV_WIDTH):
                acc = acc + cw_ref[0, t:t + 1, cols] * planes[j + t]
            outs.append(acc)
        return outs

    def block(part, cols):
        gate = conv("gate", cwg_ref, cbg_ref, 0, part, cols)
        val = conv("val", cwv_ref, cbv_ref, CONV_WIDTH - 1, part, cols)
        for j in range(CHUNK):
            o_ref[0, j, chunk_rows[part], cols] = (_silu(gate[j]) * val[j]).astype(o_ref.dtype)

    for part in range(parts):
        _for_col_blocks(tn, functools.partial(block, part))
    gate_stream.finish()
    val_stream.finish()


def _up_conv(y4, w_up, conv_w, conv_b, layer):
    bsz, _, nchunk, d = y4.shape
    d_ff = w_up.shape[2] // 2
    cm, tn = min(ROW_TILE // CHUNK, nchunk), min(COL_TILE // 2, d_ff)
    nblk = d_ff // tn
    steps = bsz * (nchunk // cm)
    return pl.pallas_call(
        functools.partial(_up_conv_kernel, layer=layer, steps=steps),
        out_shape=jax.ShapeDtypeStruct((bsz, CHUNK, nchunk, d_ff), BF16),
        grid=(nblk, bsz, nchunk // cm),
        in_specs=[
            pl.BlockSpec((1, CHUNK, cm, d), lambda n, b, c: (b, 0, c, 0)),
            HBM_SPEC,
            pl.BlockSpec((1, CONV_WIDTH, tn), lambda n, b, c: (layer, 0, n)),
            pl.BlockSpec((1, CONV_WIDTH, tn), lambda n, b, c: (layer, 0, nblk + n)),
            pl.BlockSpec((1, 1, tn), lambda n, b, c: (layer, 0, n)),
            pl.BlockSpec((1, 1, tn), lambda n, b, c: (layer, 0, nblk + n)),
        ],
        out_specs=pl.BlockSpec((1, CHUNK, cm, tn), lambda n, b, c: (b, 0, c, n)),
        scratch_shapes=([pltpu.VMEM((2 * (CONV_WIDTH - 1), SUBLANES, tn), F32)]
                        + _stream_scratch(d, tn, steps) + _stream_scratch(d, tn, steps)),
        compiler_params=_params("arbitrary", "arbitrary", "arbitrary"),
    )(y4, w_up, conv_w, conv_w, conv_b, conv_b)


def kernel(x, c, w_cond, b_cond, ada_table, norm1_g, norm2_g, w_in, ssm_a_re, ssm_a_im, ssm_log_dt, ssm_b_re, ssm_b_im, ssm_c_re, ssm_c_im, ssm_d, w_glu, b_glu, w_pool, b_pool, pool_scale, w_ssm_out, w_pool_out, w_o, w_up, conv_w, conv_b, w_down, final_g):
    bsz, seq, d = x.shape
    depth = w_in.shape[0]
    nchunk = seq // CHUNK
    rows = bsz * seq
    ssm_width = w_glu.shape[2]
    pool_width = pool_scale.shape[1]
    assert seq % CHUNK == 0 and ssm_width % SLICE_CH == 0

    c_pad = jnp.pad(c, ((0, SUBLANES - bsz), (0, 0)))
    mods = _cond(c_pad, w_cond, b_cond, ada_table)[:, :bsz].reshape(depth, bsz, N_MOD, 1, d)
    kt, p, q, dre, dim = _s5_prep(ssm_a_re, ssm_a_im, ssm_log_dt, ssm_b_re, ssm_b_im, ssm_c_re, ssm_c_im)
    d_skip = ssm_d.reshape(depth, 1, ssm_width)
    b_glu = b_glu.reshape(depth, 1, ssm_width)
    b_pool = b_pool[:, :, None, :]
    pool_scale = pool_scale.reshape(depth, 1, pool_width)
    conv_b = conv_b.reshape(depth, 1, -1)

    h = None
    for l in range(depth):
        shift1, scale1, gate1, shift2, scale2, gate2 = (mods[l, :, i] for i in range(N_MOD))
        if l == 0:
            y, h = _norm(x, norm1_g[l], scale1, shift1, in_tokens=True, copy_out=True)
            h = h.reshape(rows, d)
        else:
            y = _norm(h.reshape(bsz, CHUNK, nchunk, d), norm1_g[l], scale1, shift1)
        y = y.reshape(rows, d)
        u = _proj(y, w_in, l, 0, ssm_width + pool_width, sigmoid=False)
        gates = _proj(y, w_in, l, ssm_width + pool_width, 2 * d, sigmoid=True)
        u4 = u.reshape(bsz, CHUNK, nchunk, -1)
        y_ssm = _s5(u4, kt, p, q, dre, dim, d_skip, l).reshape(rows, ssm_width)
        o_ssm = _glu(y_ssm, w_glu, b_glu, l)
        o_pool = _pool(u4, ssm_width, w_pool, b_pool, pool_scale, l).reshape(rows, pool_width)
        merged = _merge(o_ssm, w_ssm_out, o_pool, w_pool_out, gates, l)
        h = _residual(merged, w_o, l, h, gate1, tm=ROW_TILE, tn=COL_TILE)

        y = _norm(h.reshape(bsz, CHUNK, nchunk, d), norm2_g[l], scale2, shift2)
        act = _up_conv(y, w_up, conv_w, conv_b, l).reshape(rows, -1)
        h = _residual(act, w_down, l, h, gate2, tm=ROW_TILE // 2, tn=COL_TILE // 2)

    return _norm(h.reshape(bsz, CHUNK, nchunk, d), final_g, None, None, out_tokens=True, out_dtype=x.dtype)
```

```python
import functools

import jax
import jax.numpy as jnp
from jax import lax
from jax.experimental import pallas as pl
from jax.experimental.pallas import tpu as pltpu

F32 = jnp.float32
BF16 = jnp.bfloat16

CHUNK = 16
SSM_GROUP = 16
SSM_STATE = 64
SLICE_CH = 128
SLICE_GROUPS = SLICE_CH // SSM_GROUP
SLICE_STATES = SLICE_GROUPS * SSM_STATE
POOL_WINDOWS = (2, 4, 8, 16)
CONV_WIDTH = 3
N_MOD = 6
RMS_EPS = 1e-6
LAM_RE_MAX = -1e-4
SUBLANES = 8
LANES = 128
MXU_COLS = 256
V7X_VMEM_LIMIT = 56 * 1024 * 1024
ROW_TILE = 1024
COL_TILE = 1024
MAX_WEIGHT_CHUNKS = 8
ROW_PARTS = 2
NORM_CHUNKS = 64
NORM_CHUNKS_IN = 16
NORM_CHUNKS_OUT = 32


def _params(*sem):
    return pltpu.CompilerParams(dimension_semantics=sem, vmem_limit_bytes=V7X_VMEM_LIMIT)


def _shift_rows(x, sh):
    rolled = pltpu.roll(x, sh, 0)
    idx = lax.broadcasted_iota(jnp.int32, x.shape, 0)
    return jnp.where(idx < sh, jnp.zeros_like(x), rolled)


class _WeightStream:
    def __init__(self, w_hbm, wbf_ref, stage_ref, sem, *, layer, col_block0, panel, n_panels, step, steps):
        self.w_hbm, self.wbf, self.stage, self.sem = w_hbm, wbf_ref, stage_ref, sem
        self.layer, self.col_block0 = layer, col_block0
        self.panel, self.step = panel, step
        _, self.k, self.tn = wbf_ref.shape
        self.kc = stage_ref.shape[1]
        self.n_chunks = self.k // self.kc
        self.lag = 1 if self.n_chunks + 1 <= steps else 0
        assert self.n_chunks + self.lag <= steps
        has_next = panel + 1 < n_panels
        self.starting = jnp.logical_and(has_next, step < self.n_chunks)
        self.finishing = jnp.logical_and(has_next, jnp.logical_and(step >= self.lag, step < self.n_chunks + self.lag))

    def _rows(self, chunk):
        start = chunk * self.kc
        return pl.ds(start if isinstance(start, int) else pl.multiple_of(start, self.kc), self.kc)

    def _copy(self, panel, chunk, slot):
        col = (self.col_block0 + panel) * self.tn
        cols = pl.ds(col if isinstance(col, int) else pl.multiple_of(col, self.tn), self.tn)
        return pltpu.make_async_copy(self.w_hbm.at[self.layer, self._rows(chunk), cols],
                                     self.stage.at[slot], self.sem.at[slot])

    def _cast(self, wbf_slot, chunk, slot):
        self.wbf[wbf_slot, self._rows(chunk), :] = self.stage[slot].astype(BF16)

    def begin(self):
        @pl.when(jnp.logical_and(self.panel == 0, self.step == 0))
        def _():
            self._copy(0, 0, 0).start()
            for c in range(self.n_chunks):
                if c + 1 < self.n_chunks:
                    self._copy(0, c + 1, (c + 1) % 2).start()
                self._copy(0, c, c % 2).wait()
                self._cast(0, c, c % 2)

        @pl.when(self.starting)
        def _():
            self._copy(self.panel + 1, self.step, self.step % 2).start()

        return self.wbf.at[self.panel % 2]

    def finish(self):
        @pl.when(self.finishing)
        def _():
            chunk = self.step - self.lag
            self._copy(self.panel + 1, chunk, chunk % 2).wait()
            self._cast((self.panel + 1) % 2, chunk, chunk % 2)


def _stream_scratch(k, tn, steps):
    n_chunks = 1
    while 2 * n_chunks <= min(steps - 1, MAX_WEIGHT_CHUNKS):
        n_chunks *= 2
    return [pltpu.VMEM((2, k, tn), BF16), pltpu.VMEM((2, k // n_chunks, tn), F32), pltpu.SemaphoreType.DMA((2,))]


HBM_SPEC = pl.BlockSpec(memory_space=pl.ANY)


def _sigmoid(x):
    return 0.5 * jnp.tanh(0.5 * x) + 0.5


def _silu(x):
    h = 0.5 * x
    return h * jnp.tanh(h) + h


def _for_col_blocks(width, body):
    sub = min(MXU_COLS, width)
    for j in range(width // sub):
        body(slice(j * sub, (j + 1) * sub))


def _for_blocks(rows, width, body):
    parts = ROW_PARTS if rows >= ROW_TILE else 1
    part = rows // parts
    for r in range(parts):
        _for_col_blocks(width, functools.partial(body, slice(r * part, (r + 1) * part)))


def _cond_kernel(c_ref, w_ref, b_ref, ada_ref, o_ref):
    a = jax.nn.silu(c_ref[...]).astype(BF16)
    z = jnp.dot(a, w_ref[...].astype(BF16), preferred_element_type=F32) + b_ref[...]
    o_ref[...] = z[None] + ada_ref[...]


def _cond(c_pad, w_cond, b_cond, ada_table):
    rows, d = c_pad.shape
    n = w_cond.shape[1]
    depth = ada_table.shape[0]
    tn = min(COL_TILE // 2, n)
    return pl.pallas_call(
        _cond_kernel,
        out_shape=jax.ShapeDtypeStruct((depth, rows, n), F32),
        grid=(n // tn,),
        in_specs=[
            pl.BlockSpec((rows, d), lambda i: (0, 0)),
            pl.BlockSpec((d, tn), lambda i: (0, i)),
            pl.BlockSpec((1, tn), lambda i: (0, i)),
            pl.BlockSpec((depth, 1, tn), lambda i: (0, 0, i)),
        ],
        out_specs=pl.BlockSpec((depth, rows, tn), lambda i: (0, 0, i)),
        compiler_params=_params("arbitrary"),
    )(c_pad, w_cond, b_cond.reshape(1, n), ada_table.reshape(depth, 1, n))


def _norm_kernel(*refs, modulated, in_tokens, out_tokens, copy_out):
    refs = list(refs)
    x_ref, g_ref = refs[:2]
    scale_ref, shift_ref = refs[2:4] if modulated else (None, None)
    outs = refs[4 if modulated else 2:]
    o_ref = outs[0]
    copy_ref = outs[1] if copy_out else None
    stage_ref = outs[-1] if (in_tokens or out_tokens) else None
    tc = (x_ref.shape[1] // CHUNK) if in_tokens else x_ref.shape[2]
    lane_tiles = x_ref.shape[-1] // LANES
    plane_rows = lambda j: pl.ds(j, tc, stride=CHUNK)
    if in_tokens:
        for k in range(lane_tiles):
            stage_ref[k] = x_ref[0, :, k * LANES:(k + 1) * LANES]
    for j in range(CHUNK):
        if in_tokens:
            x = jnp.concatenate([stage_ref[k, plane_rows(j), :] for k in range(lane_tiles)], axis=-1)
        else:
            x = x_ref[0, j]
        y = x * lax.rsqrt(jnp.mean(x * x, axis=-1, keepdims=True) + RMS_EPS) * g_ref[...]
        if modulated:
            y = y * (1.0 + scale_ref[0]) + shift_ref[0]
        if out_tokens:
            for k in range(lane_tiles):
                stage_ref[k, plane_rows(j), :] = y[:, k * LANES:(k + 1) * LANES]
        else:
            o_ref[0, j] = y.astype(o_ref.dtype)
        if copy_out:
            copy_ref[0, j] = x
    if out_tokens:
        for k in range(lane_tiles):
            o_ref[0, :, k * LANES:(k + 1) * LANES] = stage_ref[k].astype(o_ref.dtype)


def _norm(x, g, scale, shift, *, in_tokens=False, out_tokens=False, copy_out=False, out_dtype=BF16):
    if in_tokens:
        bsz, seq, d = x.shape
        nchunk = seq // CHUNK
    else:
        bsz, _, nchunk, d = x.shape
    tc = min(NORM_CHUNKS_IN if in_tokens else NORM_CHUNKS_OUT if out_tokens else NORM_CHUNKS, nchunk)
    tok_spec = pl.BlockSpec((1, tc * CHUNK, d), lambda b, c: (b, c, 0))
    pos_spec = pl.BlockSpec((1, CHUNK, tc, d), lambda b, c: (b, 0, c, 0))
    modulated = scale is not None
    in_specs = [tok_spec if in_tokens else pos_spec, pl.BlockSpec((1, d), lambda b, c: (0, 0))]
    args = [x, g.reshape(1, d)]
    if modulated:
        mod_spec = pl.BlockSpec((1, 1, d), lambda b, c: (b, 0, 0))
        in_specs += [mod_spec, mod_spec]
        args += [scale, shift]
    pos_shape = (bsz, CHUNK, nchunk, d)
    out_shape = [jax.ShapeDtypeStruct((bsz, nchunk * CHUNK, d) if out_tokens else pos_shape, out_dtype)]
    out_specs = [tok_spec if out_tokens else pos_spec]
    if copy_out:
        out_shape.append(jax.ShapeDtypeStruct(pos_shape, F32))
        out_specs.append(pos_spec)
    staged = in_tokens or out_tokens
    scratch = [pltpu.VMEM((d // LANES, tc * CHUNK, LANES), F32)] if staged else []
    outs = pl.pallas_call(
        functools.partial(_norm_kernel, modulated=modulated, in_tokens=in_tokens, out_tokens=out_tokens,
                          copy_out=copy_out),
        out_shape=out_shape,
        grid=(bsz, nchunk // tc),
        in_specs=in_specs,
        out_specs=out_specs,
        scratch_shapes=scratch,
        compiler_params=_params("parallel", "parallel"),
    )(*args)
    return outs if copy_out else outs[0]


def _grid_stream(w_hbm, scratch, layer, steps, col_block0=0):
    return _WeightStream(w_hbm, *scratch, layer=layer, col_block0=col_block0, panel=pl.program_id(0),
                         n_panels=pl.num_programs(0), step=pl.program_id(1), steps=steps)


def _proj_kernel(a_ref, w_hbm, o_ref, *scratch, sigmoid, layer, steps, col_block0):
    stream = _grid_stream(w_hbm, scratch, layer, steps, col_block0)
    w = stream.begin()

    def block(rows, cols):
        z = jnp.dot(a_ref[rows, :], w[:, cols], preferred_element_type=F32)
        o_ref[rows, cols] = (_sigmoid(z) if sigmoid else z).astype(o_ref.dtype)

    _for_blocks(*o_ref.shape, block)
    stream.finish()


def _proj(a, w, layer, col0, n, *, sigmoid):
    rows, k = a.shape
    tm, tn = min(ROW_TILE, rows), min(COL_TILE, n)
    return pl.pallas_call(
        functools.partial(_proj_kernel, sigmoid=sigmoid, layer=layer, steps=rows // tm, col_block0=col0 // tn),
        out_shape=jax.ShapeDtypeStruct((rows, n), BF16),
        grid=(n // tn, rows // tm),
        in_specs=[pl.BlockSpec((tm, k), lambda j, i: (i, 0)), HBM_SPEC],
        out_specs=pl.BlockSpec((tm, tn), lambda j, i: (i, j)),
        scratch_shapes=_stream_scratch(k, tn, rows // tm),
        compiler_params=_params("arbitrary", "arbitrary"),
    )(a, w)


def _discretise(a_re, a_im, log_dt):
    lam_re = jnp.minimum(a_re, LAM_RE_MAX)
    lam_im = a_im
    dt = jnp.exp(log_dt)
    mag = jnp.exp(lam_re * dt)
    abar_re = mag * jnp.cos(lam_im * dt)
    abar_im = mag * jnp.sin(lam_im * dt)
    den = lam_re * lam_re + lam_im * lam_im
    x_re = abar_re - 1.0
    f_re = (x_re * lam_re + abar_im * lam_im) / den
    f_im = (abar_im * lam_re - x_re * lam_im) / den
    return abar_re, abar_im, f_re, f_im


def _split_bf16(x):
    hi = x.astype(BF16)
    return hi, (x - hi.astype(F32)).astype(BF16)


def _dot_split(a, b):
    a_hi, a_lo = a
    b_hi, b_lo = b
    dot = functools.partial(jnp.dot, preferred_element_type=F32)
    return dot(a_hi, b_hi) + (dot(a_hi, b_lo) + dot(a_lo, b_hi))


def _s5_prep_kernel(rowp_ref, bt_ref, ct_ref, kt_ref, p_ref, q_ref, dre_ref, dim_ref):
    ns, ch = SLICE_STATES, SLICE_CH
    ar, ai, f_re, f_im = _discretise(rowp_ref[0, 0, 0:1, :], rowp_ref[0, 0, 1:2, :], rowp_ref[0, 0, 2:3, :])
    bt_re = jnp.concatenate([bt_ref[0, 0, 0]] * SLICE_GROUPS, axis=0)
    bt_im = jnp.concatenate([bt_ref[0, 0, 1]] * SLICE_GROUPS, axis=0)
    same_group = (lax.broadcasted_iota(jnp.int32, (ch, ns), 0) // SSM_GROUP
                  == lax.broadcasted_iota(jnp.int32, (ch, ns), 1) // SSM_STATE)
    e_re = jnp.where(same_group, f_re * bt_re - f_im * bt_im, 0.0)
    e_im = jnp.where(same_group, f_re * bt_im + f_im * bt_re, 0.0)
    same_group_t = (lax.broadcasted_iota(jnp.int32, (ns, ch), 0) // SSM_STATE
                    == lax.broadcasted_iota(jnp.int32, (ns, ch), 1) // SSM_GROUP)
    c_re = jnp.where(same_group_t, ct_ref[0, 0, 0], 0.0)
    c_im = jnp.where(same_group_t, ct_ref[0, 0, 1], 0.0)
    c_re_split, c_im_split = _split_bf16(c_re), _split_bf16(c_im)
    for t in range(CHUNK):
        j = CHUNK - 1 - t
        e_re_split, e_im_split = _split_bf16(e_re), _split_bf16(e_im)
        p_ref[0, 0, j * ch:(j + 1) * ch, 0:ns] = e_re_split[0]
        p_ref[0, 0, j * ch:(j + 1) * ch, ns:2 * ns] = e_im_split[0]
        kt_ref[0, 0, t] = (_dot_split(e_re_split, c_re_split) - _dot_split(e_im_split, c_im_split)).astype(BF16)
        e_re, e_im = e_re * ar - e_im * ai, e_re * ai + e_im * ar

    d_re, d_im = ar, ai
    for _ in range(CHUNK.bit_length() - 1):
        d_re, d_im = d_re * d_re - d_im * d_im, 2.0 * d_re * d_im
    p_re, p_im = d_re, d_im
    for r in range(SUBLANES):
        dre_ref[0, 0, r:r + 1, :] = p_re
        dim_ref[0, 0, r:r + 1, :] = p_im
        p_re, p_im = p_re * d_re - p_im * d_im, p_re * d_im + p_im * d_re

    arc = jnp.transpose(jnp.broadcast_to(ar, (ch, ns)))
    aic = jnp.transpose(jnp.broadcast_to(ai, (ch, ns)))
    w_re, w_im = c_re, c_im
    for i in range(CHUNK):
        w_re, w_im = w_re * arc - w_im * aic, w_re * aic + w_im * arc
        q_ref[0, 0, 0:ns, i * ch:(i + 1) * ch] = w_re.astype(BF16)
        q_ref[0, 0, ns:2 * ns, i * ch:(i + 1) * ch] = (-w_im).astype(BF16)


def _s5_prep(a_re, a_im, log_dt, b_re, b_im, c_re, c_im):
    depth, groups, nst = a_re.shape
    nsl = groups // SLICE_GROUPS
    ns, ch = SLICE_STATES, SLICE_CH
    dt_b = jnp.broadcast_to(log_dt[:, :, None], a_re.shape)
    rows = jnp.stack([a_re, a_im, dt_b], axis=2).reshape(depth, nsl, SLICE_GROUPS, 3, nst)
    rows = jnp.swapaxes(rows, 2, 3).reshape(depth, nsl, 3, ns)
    rowp = jnp.pad(rows, ((0, 0), (0, 0), (0, SUBLANES - 3), (0, 0)))
    bt = jnp.stack([b_re, b_im], axis=1).reshape(depth, 2, nsl, ns, SSM_GROUP)
    bt = jnp.transpose(bt, (0, 2, 1, 4, 3))
    ct = jnp.stack([c_re, c_im], axis=1).reshape(depth, 2, nsl, SLICE_GROUPS, SSM_GROUP, nst)
    ct = jnp.transpose(ct, (0, 2, 1, 3, 5, 4)).reshape(depth, nsl, 2, ns, SSM_GROUP)
    ct = jnp.tile(ct, (1, 1, 1, 1, SLICE_GROUPS))
    out_shapes = (
        jax.ShapeDtypeStruct((depth, nsl, CHUNK, ch, ch), BF16),
        jax.ShapeDtypeStruct((depth, nsl, CHUNK * ch, 2 * ns), BF16),
        jax.ShapeDtypeStruct((depth, nsl, 2 * ns, CHUNK * ch), BF16),
        jax.ShapeDtypeStruct((depth, nsl, SUBLANES, ns), F32),
        jax.ShapeDtypeStruct((depth, nsl, SUBLANES, ns), F32),
    )
    return pl.pallas_call(
        _s5_prep_kernel,
        out_shape=out_shapes,
        grid=(depth, nsl),
        in_specs=[
            pl.BlockSpec((1, 1, SUBLANES, ns), lambda l, s: (l, s, 0, 0)),
            pl.BlockSpec((1, 1, 2, SSM_GROUP, ns), lambda l, s: (l, s, 0, 0, 0)),
            pl.BlockSpec((1, 1, 2, ns, ch), lambda l, s: (l, s, 0, 0, 0)),
        ],
        out_specs=(
            pl.BlockSpec((1, 1, CHUNK, ch, ch), lambda l, s: (l, s, 0, 0, 0)),
            pl.BlockSpec((1, 1, CHUNK * ch, 2 * ns), lambda l, s: (l, s, 0, 0)),
            pl.BlockSpec((1, 1, 2 * ns, CHUNK * ch), lambda l, s: (l, s, 0, 0)),
            pl.BlockSpec((1, 1, SUBLANES, ns), lambda l, s: (l, s, 0, 0)),
            pl.BlockSpec((1, 1, SUBLANES, ns), lambda l, s: (l, s, 0, 0)),
        ),
        compiler_params=_params("parallel", "parallel"),
    )(rowp, bt, ct)


def _chunk_scan_prev(x_re, x_im, dre_ref, dim_ref):
    nrows, ns = x_re.shape
    nblk = nrows // SUBLANES
    x_re = x_re.reshape(nblk, SUBLANES, ns)
    x_im = x_im.reshape(nblk, SUBLANES, ns)
    row = lax.broadcasted_iota(jnp.int32, (SUBLANES, ns), 0)
    sh = 1
    while sh < SUBLANES:
        d_re = jnp.where(row >= sh, dre_ref[0, 0, sh - 1:sh, :], 0.0)
        d_im = jnp.where(row >= sh, dim_ref[0, 0, sh - 1:sh, :], 0.0)
        s_re, s_im = pltpu.roll(x_re, sh, 1), pltpu.roll(x_im, sh, 1)
        x_re, x_im = x_re + d_re * s_re - d_im * s_im, x_im + d_re * s_im + d_im * s_re
        sh *= 2
    pw_re, pw_im = dre_ref[0, 0], dim_ref[0, 0]
    first_row = row == 0
    c_re = c_im = jnp.zeros((SUBLANES, ns), F32)
    prev_re, prev_im = [], []
    for blk in range(nblk):
        b_re = x_re[blk] + pw_re * c_re - pw_im * c_im
        b_im = x_im[blk] + pw_re * c_im + pw_im * c_re
        prev_re.append(jnp.where(first_row, c_re, pltpu.roll(b_re, 1, 0)))
        prev_im.append(jnp.where(first_row, c_im, pltpu.roll(b_im, 1, 0)))
        c_re = jnp.broadcast_to(b_re[SUBLANES - 1:SUBLANES], pw_re.shape)
        c_im = jnp.broadcast_to(b_im[SUBLANES - 1:SUBLANES], pw_re.shape)
    return jnp.concatenate(prev_re, axis=0), jnp.concatenate(prev_im, axis=0)


def _s5_kernel(u_ref, kt_ref, p_ref, q_ref, dre_ref, dim_ref, dskip_ref, o_ref, toep_ref, y_ref):
    ch, ns = SLICE_CH, SLICE_STATES
    per_blk = MXU_COLS // ch
    nblk = CHUNK // per_blk

    @pl.when(pl.program_id(1) == 0)
    def _():
        for i in range(CHUNK):
            last_j = (i // per_blk + 1) * per_blk
            for j in range(last_j):
                blk = kt_ref[0, 0, i - j] if j <= i else jnp.zeros((ch, ch), BF16)
                toep_ref[j * ch:(j + 1) * ch, i * ch:(i + 1) * ch] = blk

    u_all = jnp.concatenate([u_ref[0, j] for j in range(CHUNK)], axis=-1)
    x_loc = jnp.dot(u_all, p_ref[0, 0], preferred_element_type=F32)
    for t in range(nblk):
        cols = slice(t * MXU_COLS, (t + 1) * MXU_COLS)
        k_rows = (t + 1) * MXU_COLS
        y_ref[:, cols] = jnp.dot(u_all[:, :k_rows], toep_ref[:k_rows, cols], preferred_element_type=F32)
    x_re, x_im = _chunk_scan_prev(x_loc[:, :ns], x_loc[:, ns:], dre_ref, dim_ref)
    x_prev = jnp.concatenate([x_re, x_im], axis=-1).astype(BF16)
    for t in range(nblk):
        cols = slice(t * MXU_COLS, (t + 1) * MXU_COLS)
        y_blk = y_ref[:, cols] + jnp.dot(x_prev, q_ref[0, 0, :, cols], preferred_element_type=F32)
        for s in range(per_blk):
            i = t * per_blk + s
            y = y_blk[:, s * ch:(s + 1) * ch] + u_ref[0, i].astype(F32) * dskip_ref[0]
            o_ref[0, i] = jax.nn.gelu(y).astype(o_ref.dtype)


def _s5(u4, kt, p, q, dre, dim, d_skip, layer):
    bsz, _, nchunk, _ = u4.shape
    ch, ns = SLICE_CH, SLICE_STATES
    nsl = kt.shape[1]
    assert nchunk % SUBLANES == 0
    return pl.pallas_call(
        _s5_kernel,
        out_shape=jax.ShapeDtypeStruct((bsz, CHUNK, nchunk, nsl * ch), BF16),
        grid=(nsl, bsz),
        in_specs=[
            pl.BlockSpec((1, CHUNK, nchunk, ch), lambda s, b: (b, 0, 0, s)),
            pl.BlockSpec((1, 1, CHUNK, ch, ch), lambda s, b: (layer, s, 0, 0, 0)),
            pl.BlockSpec((1, 1, CHUNK * ch, 2 * ns), lambda s, b: (layer, s, 0, 0)),
            pl.BlockSpec((1, 1, 2 * ns, CHUNK * ch), lambda s, b: (layer, s, 0, 0)),
            pl.BlockSpec((1, 1, SUBLANES, ns), lambda s, b: (layer, s, 0, 0)),
            pl.BlockSpec((1, 1, SUBLANES, ns), lambda s, b: (layer, s, 0, 0)),
            pl.BlockSpec((1, 1, ch), lambda s, b: (layer, 0, s)),
        ],
        out_specs=pl.BlockSpec((1, CHUNK, nchunk, ch), lambda s, b: (b, 0, 0, s)),
        scratch_shapes=[pltpu.VMEM((CHUNK * ch, CHUNK * ch), BF16), pltpu.VMEM((nchunk, CHUNK * ch), F32)],
        compiler_params=_params("arbitrary", "arbitrary"),
    )(u4, kt, p, q, dre, dim, d_skip)


def _glu_kernel(a_ref, w_hbm, b_ref, y_ref, o_ref, *scratch, layer, steps):
    stream = _grid_stream(w_hbm, scratch, layer, steps)
    w = stream.begin()

    def block(rows, cols):
        z = jnp.dot(a_ref[rows, :], w[:, cols], preferred_element_type=F32) + b_ref[0, :, cols]
        o_ref[rows, cols] = (y_ref[rows, cols].astype(F32) * _sigmoid(z)).astype(o_ref.dtype)

    _for_blocks(*o_ref.shape, block)
    stream.finish()


def _glu(y, w, b, layer):
    rows, k = y.shape
    n = w.shape[2]
    tm, tn = min(2 * ROW_TILE, rows), min(COL_TILE, n)
    return pl.pallas_call(
        functools.partial(_glu_kernel, layer=layer, steps=rows // tm),
        out_shape=jax.ShapeDtypeStruct((rows, n), BF16),
        grid=(n // tn, rows // tm),
        in_specs=[
            pl.BlockSpec((tm, k), lambda j, i: (i, 0)),
            HBM_SPEC,
            pl.BlockSpec((1, 1, tn), lambda j, i: (layer, 0, j)),
            pl.BlockSpec((tm, tn), lambda j, i: (i, j)),
        ],
        out_specs=pl.BlockSpec((tm, tn), lambda j, i: (i, j)),
        scratch_shapes=_stream_scratch(k, tn, rows // tm),
        compiler_params=_params("arbitrary", "arbitrary"),
    )(y, w, b, y)


def _pool_kernel(v_ref, w_ref, b_ref, s_ref, o_ref, pooled_ref):
    nchunk = v_ref.shape[2]
    group, tile = pl.program_id(1), pl.program_id(2)
    first_row = lax.broadcasted_iota(jnp.int32, (nchunk, 1), 0) == 0

    def pooled(win):
        v = [v_ref[0, j].astype(F32) for j in range(CHUNK)]
        prefix = [v[0]]
        for j in range(1, CHUNK):
            prefix.append(prefix[-1] + v[j])
        for j in range(CHUNK):
            if j >= win:
                wsum = prefix[j] - prefix[j - win]
            elif j == win - 1:
                wsum = prefix[j]
            else:
                wsum = prefix[j] + _shift_rows(prefix[CHUNK - 1] - prefix[j - win + CHUNK], 1)
            count = jnp.where(first_row, float(min(j + 1, win)), float(win))
            pooled_ref[tile, j] = (wsum / count - v[j]).astype(pooled_ref.dtype)

    for g, win in enumerate(POOL_WINDOWS):
        pl.when(group == g)(functools.partial(pooled, win))

    @pl.when(tile == pl.num_programs(2) - 1)
    def _():
        w = w_ref[0, 0].astype(BF16)
        for j in range(CHUNK):
            a = jnp.concatenate([pooled_ref[k, j] for k in range(pooled_ref.shape[0])], axis=-1)
            z = jnp.dot(a, w, preferred_element_type=F32)
            o_ref[0, j] = ((z + b_ref[0, 0]) * s_ref[0]).astype(o_ref.dtype)


def _pool(u4, col0, w, b, scale, layer):
    bsz, _, nchunk, _ = u4.shape
    ngroups, gw = w.shape[1], w.shape[2]
    lane_tiles = gw // LANES
    return pl.pallas_call(
        _pool_kernel,
        out_shape=jax.ShapeDtypeStruct((bsz, CHUNK, nchunk, ngroups * gw), BF16),
        grid=(bsz, ngroups, lane_tiles),
        in_specs=[
            pl.BlockSpec((1, CHUNK, nchunk, LANES), lambda s, g, t: (s, 0, 0, col0 // LANES + g * lane_tiles + t)),
            pl.BlockSpec((1, 1, gw, gw), lambda s, g, t: (layer, g, 0, 0)),
            pl.BlockSpec((1, 1, 1, gw), lambda s, g, t: (layer, g, 0, 0)),
            pl.BlockSpec((1, 1, gw), lambda s, g, t: (layer, 0, g)),
        ],
        out_specs=pl.BlockSpec((1, CHUNK, nchunk, gw), lambda s, g, t: (s, 0, 0, g)),
        scratch_shapes=[pltpu.VMEM((lane_tiles, CHUNK, nchunk, LANES), BF16)],
        compiler_params=_params("arbitrary", "arbitrary", "arbitrary"),
    )(u4, w, b, scale)


def _merge_kernel(a1_ref, w1_hbm, a2_ref, w2_hbm, g1_ref, g2_ref, o_ref, *scratch, layer, steps):
    stream1 = _grid_stream(w1_hbm, scratch[:3], layer, steps)
    stream2 = _grid_stream(w2_hbm, scratch[3:], layer, steps)
    w1, w2 = stream1.begin(), stream2.begin()

    def block(rows, cols):
        z1 = jnp.dot(a1_ref[rows, :], w1[:, cols], preferred_element_type=F32)
        z2 = jnp.dot(a2_ref[rows, :], w2[:, cols], preferred_element_type=F32)
        o_ref[rows, cols] = (g1_ref[rows, cols].astype(F32) * z1
                             + g2_ref[rows, cols].astype(F32) * z2).astype(o_ref.dtype)

    _for_blocks(*o_ref.shape, block)
    stream1.finish()
    stream2.finish()


def _merge(o_ssm, w_ssm_out, o_pool, w_pool_out, gates, layer):
    rows, k1 = o_ssm.shape
    k2 = o_pool.shape[1]
    n = w_ssm_out.shape[2]
    tm, tn = min(ROW_TILE, rows), min(COL_TILE, n)
    return pl.pallas_call(
        functools.partial(_merge_kernel, layer=layer, steps=rows // tm),
        out_shape=jax.ShapeDtypeStruct((rows, n), BF16),
        grid=(n // tn, rows // tm),
        in_specs=[
            pl.BlockSpec((tm, k1), lambda j, i: (i, 0)),
            HBM_SPEC,
            pl.BlockSpec((tm, k2), lambda j, i: (i, 0)),
            HBM_SPEC,
            pl.BlockSpec((tm, tn), lambda j, i: (i, j)),
            pl.BlockSpec((tm, tn), lambda j, i: (i, n // tn + j)),
        ],
        out_specs=pl.BlockSpec((tm, tn), lambda j, i: (i, j)),
        scratch_shapes=_stream_scratch(k1, tn, rows // tm) + _stream_scratch(k2, tn, rows // tm),
        compiler_params=_params("arbitrary", "arbitrary"),
    )(o_ssm, w_ssm_out, o_pool, w_pool_out, gates, gates)


def _residual_kernel(a_ref, w_hbm, h_ref, g_ref, o_ref, *scratch, layer, steps):
    stream = _grid_stream(w_hbm, scratch, layer, steps)
    w = stream.begin()

    def block(rows, cols):
        z = jnp.dot(a_ref[rows, :], w[:, cols], preferred_element_type=F32)
        o_ref[rows, cols] = h_ref[rows, cols] + g_ref[0, :, cols] * z

    _for_blocks(*o_ref.shape, block)
    stream.finish()


def _residual(a, w, layer, h, gate, *, tm, tn):
    rows, k = a.shape
    n = w.shape[2]
    rows_per_seq = rows // gate.shape[0]
    tm, tn = min(tm, rows_per_seq), min(tn, n)
    return pl.pallas_call(
        functools.partial(_residual_kernel, layer=layer, steps=rows // tm),
        out_shape=jax.ShapeDtypeStruct((rows, n), F32),
        grid=(n // tn, rows // tm),
        in_specs=[
            pl.BlockSpec((tm, k), lambda j, i: (i, 0)),
            HBM_SPEC,
            pl.BlockSpec((tm, tn), lambda j, i: (i, j)),
            pl.BlockSpec((1, 1, tn), lambda j, i: (i * tm // rows_per_seq, 0, j)),
        ],
        out_specs=pl.BlockSpec((tm, tn), lambda j, i: (i, j)),
        scratch_shapes=_stream_scratch(k, tn, rows // tm),
        input_output_aliases={2: 0},
        compiler_params=_params("arbitrary", "arbitrary"),
    )(a, w, h, gate)


def _up_conv_kernel(a_ref, w_hbm, cwg_ref, cwv_ref, cbg_ref, cbv_ref, o_ref, carry_ref, *scratch, layer, steps):
    _, _, cm, d = a_ref.shape
    tn = o_ref.shape[-1]
    seq_start = pl.program_id(2) == 0
    common = dict(layer=layer, panel=pl.program_id(0), n_panels=pl.num_programs(0),
                  step=pl.program_id(1) * pl.num_programs(2) + pl.program_id(2), steps=steps)
    gate_stream = _WeightStream(w_hbm, *scratch[:3], col_block0=0, **common)
    val_stream = _WeightStream(w_hbm, *scratch[3:], col_block0=pl.num_programs(0), **common)
    wg, wv = gate_stream.begin(), val_stream.begin()

    @pl.when(seq_start)
    def _():
        carry_ref[...] = jnp.zeros(carry_ref.shape, carry_ref.dtype)

    parts = ROW_PARTS if CHUNK * cm >= ROW_TILE else 1
    cp = cm // parts
    chunk_rows = [slice(p * cp, (p + 1) * cp) for p in range(parts)]

    products = {}

    def matmuls(part, cols):
        a = a_ref[0, :, chunk_rows[part], :].reshape(CHUNK * cp, d)
        for name, w in (("gate", wg), ("val", wv)):
            products[name, part, cols.start] = jnp.dot(a, w[:, cols], preferred_element_type=F32)

    for part in range(parts):
        _for_col_blocks(tn, functools.partial(matmuls, part))

    def conv(name, cw_ref, cb_ref, slot, part, cols):
        sub = cols.stop - cols.start
        first_row = lax.broadcasted_iota(jnp.int32, (cp, sub), 0) == 0
        r = products[name, part, cols.start].reshape(CHUNK, cp, sub)
        prev = []
        for t in range(CONV_WIDTH - 1):
            plane = r[CHUNK - (CONV_WIDTH - 1) + t]
            prev.append(jnp.where(first_row, carry_ref[slot + t, 0:1, cols], pltpu.roll(plane, 1, 0)))
            carry_ref[slot + t, :, cols] = jnp.broadcast_to(plane[cp - 1:cp, :], (SUBLANES, sub))
        planes = prev + [r[j] for j in range(CHUNK)]
        outs = []
        for j in range(CHUNK):
            acc = cb_ref[0, :, cols]
            for t in range(CONV_WIDTH):
                acc = acc + cw_ref[0, t:t + 1, cols] * planes[j + t]
            outs.append(acc)
        return outs

    def block(part, cols):
        gate = conv("gate", cwg_ref, cbg_ref, 0, part, cols)
        val = conv("val", cwv_ref, cbv_ref, CONV_WIDTH - 1, part, cols)
        for j in range(CHUNK):
            o_ref[0, j, chunk_rows[part], cols] = (_silu(gate[j]) * val[j]).astype(o_ref.dtype)

    for part in range(parts):
        _for_col_blocks(tn, functools.partial(block, part))
    gate_stream.finish()
    val_stream.finish()


def _up_conv(y4, w_up, conv_w, conv_b, layer):
    bsz, _, nchunk, d = y4.shape
    d_ff = w_up.shape[2] // 2
    cm, tn = min(ROW_TILE // CHUNK, nchunk), min(COL_TILE // 2, d_ff)
    nblk = d_ff // tn
    steps = bsz * (nchunk // cm)
    return pl.pallas_call(
        functools.partial(_up_conv_kernel, layer=layer, steps=steps),
        out_shape=jax.ShapeDtypeStruct((bsz, CHUNK, nchunk, d_ff), BF16),
        grid=(nblk, bsz, nchunk // cm),
        in_specs=[
            pl.BlockSpec((1, CHUNK, cm, d), lambda n, b, c: (b, 0, c, 0)),
            HBM_SPEC,
            pl.BlockSpec((1, CONV_WIDTH, tn), lambda n, b, c: (layer, 0, n)),
            pl.BlockSpec((1, CONV_WIDTH, tn), lambda n, b, c: (layer, 0, nblk + n)),
            pl.BlockSpec((1, 1, tn), lambda n, b, c: (layer, 0, n)),
            pl.BlockSpec((1, 1, tn), lambda n, b, c: (layer, 0, nblk + n)),
        ],
        out_specs=pl.BlockSpec((1, CHUNK, cm, tn), lambda n, b, c: (b, 0, c, n)),
        scratch_shapes=([pltpu.VMEM((2 * (CONV_WIDTH - 1), SUBLANES, tn), F32)]
                        + _stream_scratch(d, tn, steps) + _stream_scratch(d, tn, steps)),
        compiler_params=_params("arbitrary", "arbitrary", "arbitrary"),
    )(y4, w_up, conv_w, conv_w, conv_b, conv_b)


def kernel(x, c, w_cond, b_cond, ada_table, norm1_g, norm2_g, w_in, ssm_a_re, ssm_a_im, ssm_log_dt, ssm_b_re, ssm_b_im, ssm_c_re, ssm_c_im, ssm_d, w_glu, b_glu, w_pool, b_pool, pool_scale, w_ssm_out, w_pool_out, w_o, w_up, conv_w, conv_b, w_down, final_g):
    bsz, seq, d = x.shape
    depth = w_in.shape[0]
    nchunk = seq // CHUNK
    rows = bsz * seq
    ssm_width = w_glu.shape[2]
    pool_width = pool_scale.shape[1]
    assert seq % CHUNK == 0 and ssm_width % SLICE_CH == 0

    c_pad = jnp.pad(c, ((0, SUBLANES - bsz), (0, 0)))
    mods = _cond(c_pad, w_cond, b_cond, ada_table)[:, :bsz].reshape(depth, bsz, N_MOD, 1, d)
    kt, p, q, dre, dim = _s5_prep(ssm_a_re, ssm_a_im, ssm_log_dt, ssm_b_re, ssm_b_im, ssm_c_re, ssm_c_im)
    d_skip = ssm_d.reshape(depth, 1, ssm_width)
    b_glu = b_glu.reshape(depth, 1, ssm_width)
    b_pool = b_pool[:, :, None, :]
    pool_scale = pool_scale.reshape(depth, 1, pool_width)
    conv_b = conv_b.reshape(depth, 1, -1)

    h = None
    for l in range(depth):
        shift1, scale1, gate1, shift2, scale2, gate2 = (mods[l, :, i] for i in range(N_MOD))
        if l == 0:
            y, h = _norm(x, norm1_g[l], scale1, shift1, in_tokens=True, copy_out=True)
            h = h.reshape(rows, d)
        else:
            y = _norm(h.reshape(bsz, CHUNK, nchunk, d), norm1_g[l], scale1, shift1)
        y = y.reshape(rows, d)
        u = _proj(y, w_in, l, 0, ssm_width + pool_width, sigmoid=False)
        gates = _proj(y, w_in, l, ssm_width + pool_width, 2 * d, sigmoid=True)
        u4 = u.reshape(bsz, CHUNK, nchunk, -1)
        y_ssm = _s5(u4, kt, p, q, dre, dim, d_skip, l).reshape(rows, ssm_width)
        o_ssm = _glu(y_ssm, w_glu, b_glu, l)
        o_pool = _pool(u4, ssm_width, w_pool, b_pool, pool_scale, l).reshape(rows, pool_width)
        merged = _merge(o_ssm, w_ssm_out, o_pool, w_pool_out, gates, l)
        h = _residual(merged, w_o, l, h, gate1, tm=ROW_TILE, tn=COL_TILE)

        y = _norm(h.reshape(bsz, CHUNK, nchunk, d), norm2_g[l], scale2, shift2)
        act = _up_conv(y, w_up, conv_w, conv_b, l).reshape(rows, -1)
        h = _residual(act, w_down, l, h, gate2, tm=ROW_TILE // 2, tn=COL_TILE // 2)

    return _norm(h.reshape(bsz, CHUNK, nchunk, d), final_g, None, None, out_tokens=True, out_dtype=x.dtype)
```
